```python
import math
import jax, jax.numpy as jnp
from jax import lax
import numpy as np

D_MODEL = 1024
BATCH = 2
SEQ = 16384
DEPTH = 2

CHUNK = 64
N_META = 16
Q_BLOCK = 128
N_A = DEPTH // 2
N_B = DEPTH - N_A
ALPHA = (2.0 * DEPTH) ** 0.25
BETA = (8.0 * DEPTH) ** -0.25
CONV_W = 31
N_HEADS = 8
HEAD_DIM = D_MODEL // N_HEADS // 2
V_DIM = 2 * HEAD_DIM
ROT_DIM = HEAD_DIM // 4
ROPE_THETA = 500000.0
N_GROUPS = 4
EXPERTS_PER_GROUP = 4
N_EXPERTS = N_GROUPS * EXPERTS_PER_GROUP
TOP_K = 2
D_EXPERT = D_MODEL // 2
EPS = 1e-5

kernel_name = "yoco_conformer_diffattn_hmoe_trunk"


def layernorm(x, g, b):
    xf = x.astype(jnp.float32)
    mu = jnp.mean(xf, axis=-1, keepdims=True)
    var = jnp.mean(jnp.square(xf - mu), axis=-1, keepdims=True)
    y = (xf - mu) * lax.rsqrt(var + EPS)
    return (y * g.astype(jnp.float32) + b.astype(jnp.float32)).astype(x.dtype)


def rmsnorm(x, g):
    xf = x.astype(jnp.float32)
    y = xf * lax.rsqrt(jnp.mean(jnp.square(xf), axis=-1, keepdims=True) + EPS)
    return (y * g.astype(jnp.float32)).astype(x.dtype)


def chunk_ids(n):
    p = jnp.arange(n, dtype=jnp.int32)
    return jnp.where(p < N_META, 0, 1 + (p - N_META) // CHUNK)


def rotary_tables(n):
    pos = jnp.arange(n, dtype=jnp.float32)
    inv = ROPE_THETA ** (-jnp.arange(0, ROT_DIM, 2, dtype=jnp.float32) / ROT_DIM)
    ang = pos[:, None] * inv[None, :]
    return jnp.cos(ang), jnp.sin(ang)


def apply_partial_rope(t, cos, sin):
    half = ROT_DIM // 2
    c = cos[:, None, None, :]
    s = sin[:, None, None, :]
    tf = t[..., :ROT_DIM].astype(jnp.float32)
    t1, t2 = tf[..., :half], tf[..., half:]
    rot = jnp.concatenate([t1 * c - t2 * s, t1 * s + t2 * c], axis=-1).astype(t.dtype)
    return jnp.concatenate([rot, t[..., ROT_DIM:]], axis=-1)


def conformer_conv(x, w_pw1, b_pw1, w_dw, b_dw, ln_g, ln_b, w_pw2, b_pw2):
    h = x @ w_pw1 + b_pw1
    a, g = jnp.split(h, 2, axis=-1)
    h = a * jax.nn.sigmoid(g)
    hp = jnp.pad(h, ((0, 0), (CONV_W - 1, 0), (0, 0)))
    h = lax.conv_general_dilated(hp, w_dw[:, None, :], window_strides=(1,), padding='VALID',
                                 dimension_numbers=('NWC', 'WIO', 'NWC'),
                                 feature_group_count=D_MODEL) + b_dw
    h = jax.nn.silu(layernorm(h, ln_g, ln_b))
    return h @ w_pw2 + b_pw2


def hier_moe(x2d, w_group, b_group, w_expert, b_expert, w1, w3, w2):
    T = x2d.shape[0]
    g_prob = jax.nn.softmax((x2d @ w_group + b_group).astype(jnp.float32), axis=-1)
    g_w, g_idx = lax.top_k(g_prob, 1)
    e_logits = (x2d @ w_expert + b_expert).astype(jnp.float32).reshape(T, N_GROUPS, EXPERTS_PER_GROUP)
    e_logits = jnp.take_along_axis(e_logits, g_idx[:, :, None], axis=1)[:, 0]
    e_prob = jax.nn.softmax(e_logits, axis=-1)
    e_w, e_idx = lax.top_k(e_prob, TOP_K)
    e_w = e_w / jnp.sum(e_w, axis=-1, keepdims=True)
    w = (g_w * e_w).astype(x2d.dtype)
    flat_idx = g_idx * EXPERTS_PER_GROUP + e_idx
    combine = jnp.sum(jax.nn.one_hot(flat_idx, N_EXPERTS, dtype=x2d.dtype) * w[..., None], axis=1)

    def expert_step(acc, p):
        w1e, w3e, w2e, c = p
        h = jax.nn.silu(x2d @ w1e) * (x2d @ w3e)
        return acc + c[:, None] * (h @ w2e), None

    out, _ = lax.scan(expert_step, jnp.zeros_like(x2d), (w1, w3, w2, combine.T))
    return out


def diff_attention(x, k, v, w_q, lam_q1, lam_k1, lam_q2, lam_k2, subln_g, w_o, lam_init, cos, sin):
    B, L, _ = x.shape
    q = apply_partial_rope((x @ w_q).reshape(B, L, N_HEADS, 2, HEAD_DIM), cos, sin)
    lam = (jnp.exp(jnp.sum((lam_q1 * lam_k1).astype(jnp.float32)))
           - jnp.exp(jnp.sum((lam_q2 * lam_k2).astype(jnp.float32))) + lam_init)
    n_blocks = -(-L // Q_BLOCK)
    l_pad = n_blocks * Q_BLOCK
    q = jnp.pad(q, ((0, 0), (0, l_pad - L), (0, 0), (0, 0), (0, 0)))
    qb = q.reshape(B, n_blocks, Q_BLOCK, N_HEADS, 2, HEAD_DIM).transpose(1, 0, 2, 3, 4, 5)
    cid_q = chunk_ids(l_pad).reshape(n_blocks, Q_BLOCK)
    cid_k = chunk_ids(L)
    scale = HEAD_DIM ** -0.5

    def block(args):
        qi, ci = args
        s = jnp.einsum('bqhcd,bkhcd->bhcqk', qi, k).astype(jnp.float32) * scale
        mask = cid_k[None, :] <= ci[:, None]
        s = jnp.where(mask[None, None, None], s, jnp.finfo(jnp.float32).min)
        p = jax.nn.softmax(s, axis=-1)
        pd = (p[:, :, 0] - lam * p[:, :, 1]).astype(v.dtype)
        return jnp.einsum('bhqk,bkhe->bqhe', pd, v)

    o = lax.map(block, (qb, cid_q))
    o = o.transpose(1, 0, 2, 3, 4).reshape(B, l_pad, N_HEADS, V_DIM)[:, :L]
    o = rmsnorm(o, subln_g) * (1.0 - lam_init)
    return o.reshape(B, L, N_HEADS * V_DIM) @ w_o


def setup_inputs(seed: int = 0) -> dict:
    key = jax.random.key(seed)
    ks = jax.random.split(key, 32)
    f32 = jnp.float32

    def nrm(k, shape, scale):
        return jax.random.normal(k, shape, f32) * scale

    d = D_MODEL
    qkw = N_HEADS * 2 * HEAD_DIM
    vw = N_HEADS * V_DIM
    return {
        "x": nrm(ks[0], (BATCH, SEQ, d), 1.0),
        "meta_tokens": nrm(ks[1], (N_META, d), 1.0),
        "conv_w_pw1": nrm(ks[2], (N_A, d, 2 * d), d ** -0.5),
        "conv_b_pw1": nrm(ks[3], (N_A, 2 * d), 0.01),
        "conv_w_dw": nrm(ks[4], (N_A, CONV_W, d), CONV_W ** -0.5),
        "conv_b_dw": nrm(ks[5], (N_A, d), 0.01),
        "conv_ln_g": 1.0 + nrm(ks[6], (N_A, d), 0.02),
        "conv_ln_b": nrm(ks[7], (N_A, d), 0.01),
        "conv_w_pw2": nrm(ks[8], (N_A, d, d), BETA * d ** -0.5),
        "conv_b_pw2": nrm(ks[9], (N_A, d), 0.01),
        "kv_w_k": nrm(ks[10], (d, qkw), d ** -0.5),
        "kv_w_v": nrm(ks[11], (d, vw), BETA * d ** -0.5),
        "attn_w_q": nrm(ks[12], (N_B, d, qkw), d ** -0.5),
        "attn_lam_q1": nrm(ks[13], (N_B, HEAD_DIM), 0.1),
        "attn_lam_k1": nrm(ks[14], (N_B, HEAD_DIM), 0.1),
        "attn_lam_q2": nrm(ks[15], (N_B, HEAD_DIM), 0.1),
        "attn_lam_k2": nrm(ks[16], (N_B, HEAD_DIM), 0.1),
        "attn_subln_g": 1.0 + nrm(ks[17], (N_B, V_DIM), 0.02),
        "attn_w_o": nrm(ks[18], (N_B, vw, d), BETA * vw ** -0.5),
        "post_ln_g": 1.0 + nrm(ks[19], (DEPTH, 2, d), 0.02),
        "post_ln_b": nrm(ks[20], (DEPTH, 2, d), 0.01),
        "moe_w_group": nrm(ks[21], (DEPTH, d, N_GROUPS), d ** -0.5),
        "moe_b_group": nrm(ks[22], (DEPTH, N_GROUPS), 0.01),
        "moe_w_expert": nrm(ks[23], (DEPTH, d, N_EXPERTS), d ** -0.5),
        "moe_b_expert": nrm(ks[24], (DEPTH, N_EXPERTS), 0.01),
        "moe_w1": nrm(ks[25], (DEPTH, N_EXPERTS, d, D_EXPERT), d ** -0.5),
        "moe_w3": nrm(ks[26], (DEPTH, N_EXPERTS, d, D_EXPERT), d ** -0.5),
        "moe_w2": nrm(ks[27], (DEPTH, N_EXPERTS, D_EXPERT, d), BETA * D_EXPERT ** -0.5),
    }


def reference(x, meta_tokens, conv_w_pw1, conv_b_pw1, conv_w_dw, conv_b_dw, conv_ln_g, conv_ln_b,
              conv_w_pw2, conv_b_pw2, kv_w_k, kv_w_v, attn_w_q, attn_lam_q1, attn_lam_k1,
              attn_lam_q2, attn_lam_k2, attn_subln_g, attn_w_o, post_ln_g, post_ln_b,
              moe_w_group, moe_b_group, moe_w_expert, moe_b_expert, moe_w1, moe_w3, moe_w2):
    B, S, D = x.shape
    meta = jnp.broadcast_to(meta_tokens.astype(x.dtype)[None], (B, N_META, D))
    h = jnp.concatenate([meta, x], axis=1)
    L = h.shape[1]
    cos, sin = rotary_tables(L)
    k_shared = None
    v_shared = None
    for l in range(DEPTH):
        if l < N_A:
            a = l
            m = conformer_conv(h, conv_w_pw1[a], conv_b_pw1[a], conv_w_dw[a], conv_b_dw[a],
                               conv_ln_g[a], conv_ln_b[a], conv_w_pw2[a], conv_b_pw2[a])
        else:
            bi = l - N_A
            lam_init = 0.8 - 0.6 * math.exp(-0.3 * l)
            m = diff_attention(h, k_shared, v_shared, attn_w_q[bi], attn_lam_q1[bi], attn_lam_k1[bi],
                               attn_lam_q2[bi], attn_lam_k2[bi], attn_subln_g[bi], attn_w_o[bi],
                               lam_init, cos, sin)
        h = layernorm(ALPHA * h + m, post_ln_g[l, 0], post_ln_b[l, 0])
        f = hier_moe(h.reshape(B * L, D), moe_w_group[l], moe_b_group[l], moe_w_expert[l],
                     moe_b_expert[l], moe_w1[l], moe_w3[l], moe_w2[l]).reshape(B, L, D)
        h = layernorm(ALPHA * h + f, post_ln_g[l, 1], post_ln_b[l, 1])
        if l == N_A - 1:
            k_shared = apply_partial_rope((h @ kv_w_k).reshape(B, L, N_HEADS, 2, HEAD_DIM), cos, sin)
            v_shared = (h @ kv_w_v).reshape(B, L, N_HEADS, V_DIM)
    return h[:, N_META:]
```

```python
import functools
import math

import jax
import jax.numpy as jnp
from jax import lax
from jax.experimental import pallas as pl
from jax.experimental.pallas import tpu as pltpu

D_MODEL = 1024
N_META = 16
CHUNK = 64
DEPTH = 2
ALPHA = (2.0 * DEPTH) ** 0.25
CONV_W = 31
N_HEADS = 8
HEAD_DIM = 64
V_DIM = 128
ROT_DIM = 16
ROPE_THETA = 500000.0
N_GROUPS = 4
EXPERTS_PER_GROUP = 4
N_EXPERTS = 16
D_EXPERT = 512
EPS = 1e-5

TM = 256
PAD0 = TM - N_META
HALO = 32
TG = 256
LANES = 128
NEG = -1e30
VMEM_LIMIT = 48 * 1024 * 1024


def _cparams(sem):
    return pltpu.CompilerParams(dimension_semantics=sem, vmem_limit_bytes=VMEM_LIMIT)


def _layernorm(x, g, b):
    mu = jnp.mean(x, axis=-1, keepdims=True)
    xc = x - mu
    var = jnp.mean(xc * xc, axis=-1, keepdims=True)
    return xc * lax.rsqrt(var + EPS) * g + b


def _sigmoid(x):
    return 1.0 / (1.0 + jnp.exp(-x))


def _front_tile(meta_ref):
    return jnp.concatenate([jnp.zeros((PAD0, D_MODEL), jnp.float32), meta_ref[...]], axis=0)


def _stream_tile(i, x_ref, meta_ref):
    return jnp.where(i == 0, _front_tile(meta_ref), x_ref[0])


def _conv_in_kernel(x_ref, meta_ref, w_ref, b_ref, u_ref):
    i = pl.program_id(1)
    xt = _stream_tile(i, x_ref, meta_ref)
    h = jnp.dot(xt.astype(jnp.bfloat16), w_ref[...], preferred_element_type=jnp.float32) + b_ref[...]
    u = h[:, :D_MODEL] * _sigmoid(h[:, D_MODEL:])
    row = lax.broadcasted_iota(jnp.int32, (TM, 1), 0)
    u_ref[0] = jnp.where((i == 0) & (row < PAD0), 0.0, u)


def _conv_mid_kernel(ucur_ref, uprev_ref, x_ref, meta_ref, wdw_ref, bdw_ref, lng_ref, lnb_ref,
                     w2_ref, b2_ref, pg_ref, pb_ref, h_ref, win_ref, conv_ref):
    i = pl.program_id(1)
    win_ref[0:HALO, :] = jnp.where(i == 0, 0.0, uprev_ref[0])
    win_ref[HALO:HALO + TM, :] = ucur_ref[0]
    off = HALO - (CONV_W - 1)
    for c in range(D_MODEL // LANES):
        cs = slice(c * LANES, (c + 1) * LANES)
        acc = jnp.zeros((TM, LANES), jnp.float32)
        for j in range(CONV_W):
            acc = acc + win_ref[off + j:off + j + TM, cs] * wdw_ref[j:j + 1, cs]
        conv_ref[:, cs] = acc
    y = conv_ref[...] + bdw_ref[...]
    y = _layernorm(y, lng_ref[...], lnb_ref[...])
    y = y * _sigmoid(y)
    m = jnp.dot(y.astype(jnp.bfloat16), w2_ref[...], preferred_element_type=jnp.float32) + b2_ref[...]
    h0 = _stream_tile(i, x_ref, meta_ref)
    h_ref[0] = _layernorm(ALPHA * h0 + m, pg_ref[...], pb_ref[...])


def _conv_layer(x, meta, w1, b1, wdw, bdw, lng, lnb, w2, b2, pg, pb):
    B, S, D = x.shape
    nt = S // TM + 1
    lp = nt * TM
    xmap = lambda b, i: (b, jnp.maximum(i - 1, 0), 0)
    full2 = lambda b, i: (0, 0)
    u = pl.pallas_call(
        _conv_in_kernel,
        grid=(B, nt),
        in_specs=[
            pl.BlockSpec((1, TM, D), xmap),
            pl.BlockSpec((N_META, D), full2),
            pl.BlockSpec((D, 2 * D), full2),
            pl.BlockSpec((1, 2 * D), full2),
        ],
        out_specs=pl.BlockSpec((1, TM, D), lambda b, i: (b, i, 0)),
        out_shape=jax.ShapeDtypeStruct((B, lp, D), jnp.float32),
        compiler_params=_cparams(("parallel", "parallel")),
        name="conv_in",
    )(x, meta, w1.astype(jnp.bfloat16), b1.reshape(1, -1))
    wdw_p = jnp.concatenate([wdw, jnp.zeros((HALO - CONV_W, D), wdw.dtype)], axis=0)
    vec = lambda a: a.reshape(1, D)
    h1 = pl.pallas_call(
        _conv_mid_kernel,
        grid=(B, nt),
        in_specs=[
            pl.BlockSpec((1, TM, D), lambda b, i: (b, i, 0)),
            pl.BlockSpec((1, HALO, D), lambda b, i: (b, jnp.maximum(i * (TM // HALO) - 1, 0), 0)),
            pl.BlockSpec((1, TM, D), xmap),
            pl.BlockSpec((N_META, D), full2),
            pl.BlockSpec((HALO, D), full2),
            pl.BlockSpec((1, D), full2),
            pl.BlockSpec((1, D), full2),
            pl.BlockSpec((1, D), full2),
            pl.BlockSpec((D, D), full2),
            pl.BlockSpec((1, D), full2),
            pl.BlockSpec((1, D), full2),
            pl.BlockSpec((1, D), full2),
        ],
        out_specs=pl.BlockSpec((1, TM, D), lambda b, i: (b, i, 0)),
        out_shape=jax.ShapeDtypeStruct((B, lp, D), jnp.float32),
        scratch_shapes=[pltpu.VMEM((HALO + TM, D), jnp.float32), pltpu.VMEM((TM, D), jnp.float32)],
        compiler_params=_cparams(("parallel", "parallel")),
        name="conv_mid",
    )(u, u, x, meta, wdw_p, vec(bdw), vec(lng), vec(lnb), w2.astype(jnp.bfloat16), vec(b2), vec(pg), vec(pb))
    return h1


def _route_kernel(h_ref, whi_ref, wlo_ref, b_ref, info_ref, cnt_ref, carry_ref):
    step = pl.program_id(0)

    @pl.when(step == 0)
    def _():
        carry_ref[...] = jnp.zeros_like(carry_ref)

    h = h_ref[...]
    hhi = h.astype(jnp.bfloat16)
    hlo = (h - hhi.astype(jnp.float32)).astype(jnp.bfloat16)
    lg = (jnp.dot(hhi, whi_ref[...], preferred_element_type=jnp.float32)
          + jnp.dot(hlo, whi_ref[...], preferred_element_type=jnp.float32)
          + jnp.dot(hhi, wlo_ref[...], preferred_element_type=jnp.float32)) + b_ref[...]
    lane_i = lax.broadcasted_iota(jnp.int32, (TM, LANES), 1)
    lane = lane_i.astype(jnp.float32)

    def first_argmax(v, vmax):
        return jnp.min(jnp.where(v == vmax, lane, float(LANES)), axis=1, keepdims=True)

    gmask = lane_i < N_GROUPS
    glog = jnp.where(gmask, lg, NEG)
    gmax = jnp.max(glog, axis=1, keepdims=True)
    gidx = first_argmax(glog, gmax)
    gsum = jnp.sum(jnp.where(gmask, jnp.exp(glog - gmax), 0.0), axis=1, keepdims=True)
    g_w = 1.0 / gsum
    elane = lane_i - N_GROUPS
    egrp = (elane >> 2).astype(jnp.float32)
    emask = (elane >= 0) & (elane < N_EXPERTS) & (egrp == gidx)
    elog = jnp.where(emask, lg, NEG)
    e1 = jnp.max(elog, axis=1, keepdims=True)
    i1 = first_argmax(elog, e1)
    elog2 = jnp.where(lane == i1, NEG, elog)
    e2 = jnp.max(elog2, axis=1, keepdims=True)
    i2 = first_argmax(elog2, e2)
    d = jnp.exp(e2 - e1)
    w1 = g_w / (1.0 + d)
    w2 = g_w * d / (1.0 + d)
    x1 = i1 - float(N_GROUPS)
    x2 = i2 - float(N_GROUPS)
    oh1 = jnp.where(lane == x1, 1.0, 0.0)
    oh2 = jnp.where(lane == x2, 1.0, 0.0)
    r_i = lax.broadcasted_iota(jnp.int32, (TM, TM), 0)
    c_i = lax.broadcasted_iota(jnp.int32, (TM, TM), 1)
    ltri = jnp.where(c_i < r_i, 1.0, 0.0).astype(jnp.bfloat16)
    pre1 = jnp.dot(ltri, oh1.astype(jnp.bfloat16), preferred_element_type=jnp.float32)
    pre2 = jnp.dot(ltri, oh2.astype(jnp.bfloat16), preferred_element_type=jnp.float32)
    cnt1 = jnp.sum(oh1, axis=0, keepdims=True)
    cnt2 = jnp.sum(oh2, axis=0, keepdims=True)
    carry = carry_ref[0:1, :]
    rank1 = jnp.sum(oh1 * (carry + pre1), axis=1, keepdims=True)
    rank2 = jnp.sum(oh2 * (carry + cnt1 + pre2), axis=1, keepdims=True)
    new_carry = carry + cnt1 + cnt2
    carry_ref[...] = jnp.broadcast_to(new_carry, carry_ref.shape)
    cnt_ref[...] = jnp.broadcast_to(new_carry, cnt_ref.shape)
    info = jnp.where(lane_i == 0, x1, 0.0)
    info = jnp.where(lane_i == 1, x2, info)
    info = jnp.where(lane_i == 2, w1, info)
    info = jnp.where(lane_i == 3, w2, info)
    info = jnp.where(lane_i == 4, rank1, info)
    info = jnp.where(lane_i == 5, rank2, info)
    info_ref[...] = info


def _route(h2d, w_group, b_group, w_expert, b_expert):
    T, D = h2d.shape
    wr = jnp.zeros((D, LANES), jnp.float32)
    wr = wr.at[:, :N_GROUPS].set(w_group).at[:, N_GROUPS:N_GROUPS + N_EXPERTS].set(w_expert)
    br = jnp.zeros((1, LANES), jnp.float32)
    br = br.at[0, :N_GROUPS].set(b_group).at[0, N_GROUPS:N_GROUPS + N_EXPERTS].set(b_expert)
    whi = wr.astype(jnp.bfloat16)
    wlo = (wr - whi.astype(jnp.float32)).astype(jnp.bfloat16)
    info, cnt = pl.pallas_call(
        _route_kernel,
        grid=(T // TM,),
        in_specs=[
            pl.BlockSpec((TM, D), lambda i: (i, 0)),
            pl.BlockSpec((D, LANES), lambda i: (0, 0)),
            pl.BlockSpec((D, LANES), lambda i: (0, 0)),
            pl.BlockSpec((1, LANES), lambda i: (0, 0)),
        ],
        out_specs=[
            pl.BlockSpec((TM, LANES), lambda i: (i, 0)),
            pl.BlockSpec((8, LANES), lambda i: (0, 0)),
        ],
        out_shape=[
            jax.ShapeDtypeStruct((T, LANES), jnp.float32),
            jax.ShapeDtypeStruct((8, LANES), jnp.float32),
        ],
        scratch_shapes=[pltpu.VMEM((8, LANES), jnp.float32)],
        compiler_params=_cparams(("arbitrary",)),
        name="route",
    )(h2d, whi, wlo, br)
    return info, cnt


def _dispatch_plan(info, cnt, n_rows):
    T = info.shape[0]
    ex = info[:, 0:2].astype(jnp.int32)
    rank = info[:, 4:6].astype(jnp.int32)
    counts = cnt[0, :N_EXPERTS].astype(jnp.int32)
    padded = ((counts + TG - 1) // TG) * TG
    ends = jnp.cumsum(padded)
    starts = ends - padded
    pos = starts[ex] + rank
    tok = jnp.broadcast_to(jnp.arange(T, dtype=jnp.int32)[:, None], (T, 2))
    src = jnp.zeros((n_rows,), jnp.int32).at[pos.reshape(-1)].set(tok.reshape(-1))
    tile_start = jnp.arange(n_rows // TG, dtype=jnp.int32) * TG
    tile_expert = jnp.minimum(
        jnp.sum((tile_start[:, None] >= ends[None, :]).astype(jnp.int32), axis=1), N_EXPERTS - 1)
    n_used = (ends[-1] // TG).astype(jnp.int32).reshape(1)
    return pos, src, tile_expert.astype(jnp.int32), n_used


def _gather_kernel(idx_ref, table_ref, out_ref, sem):
    def row_copy(r):
        return pltpu.make_async_copy(table_ref.at[pl.ds(idx_ref[0, 0, r], 1), :],
                                     out_ref.at[pl.ds(r, 1), :], sem)

    def start(r, c):
        row_copy(r).start()
        return c

    def wait(r, c):
        row_copy(r).wait()
        return c

    lax.fori_loop(0, TG, start, 0)
    lax.fori_loop(0, TG, wait, 0)


def _gather_rows(table, idx):
    n = idx.shape[0]
    D = table.shape[1]
    return pl.pallas_call(
        _gather_kernel,
        grid=(n // TG,),
        in_specs=[
            pl.BlockSpec((1, 1, TG), lambda i: (i, 0, 0), memory_space=pltpu.SMEM),
            pl.BlockSpec(memory_space=pl.ANY),
        ],
        out_specs=pl.BlockSpec((TG, D), lambda i: (i, 0)),
        out_shape=jax.ShapeDtypeStruct((n, D), table.dtype),
        scratch_shapes=[pltpu.SemaphoreType.DMA(())],
        compiler_params=_cparams(("arbitrary",)),
        name="gather_rows",
    )(idx.reshape(n // TG, 1, TG), table)


def _gmm_kernel(te_ref, nu_ref, x_ref, w1_ref, w3_ref, w2_ref, y_ref):
    j = pl.program_id(0)

    @pl.when(j < nu_ref[0])
    def _():
        x = x_ref[...].astype(jnp.bfloat16)
        a = jnp.dot(x, w1_ref[0], preferred_element_type=jnp.float32)
        g = jnp.dot(x, w3_ref[0], preferred_element_type=jnp.float32)
        hh = (a * _sigmoid(a) * g).astype(jnp.bfloat16)
        y_ref[...] = jnp.dot(hh, w2_ref[0], preferred_element_type=jnp.float32)

    @pl.when(j >= nu_ref[0])
    def _():
        y_ref[...] = jnp.zeros_like(y_ref)


def _gmm(xs, w1, w3, w2, tile_expert, n_used):
    n, D = xs.shape
    de = w1.shape[2]
    grid_spec = pltpu.PrefetchScalarGridSpec(
        num_scalar_prefetch=2,
        grid=(n // TG,),
        in_specs=[
            pl.BlockSpec((TG, D), lambda j, te, nu: (j, 0)),
            pl.BlockSpec((1, D, de), lambda j, te, nu: (te[j], 0, 0)),
            pl.BlockSpec((1, D, de), lambda j, te, nu: (te[j], 0, 0)),
            pl.BlockSpec((1, de, D), lambda j, te, nu: (te[j], 0, 0)),
        ],
        out_specs=pl.BlockSpec((TG, D), lambda j, te, nu: (j, 0)),
    )
    return pl.pallas_call(
        _gmm_kernel,
        grid_spec=grid_spec,
        out_shape=jax.ShapeDtypeStruct((n, D), jnp.float32),
        compiler_params=_cparams(("arbitrary",)),
        name="expert_mlp",
    )(tile_expert, n_used, xs, w1, w3, w2)


def _combine_kernel(h_ref, y1_ref, y2_ref, info_ref, g_ref, b_ref, o_ref):
    info = info_ref[...]
    f = info[:, 2:3] * y1_ref[...] + info[:, 3:4] * y2_ref[...]
    o_ref[...] = _layernorm(ALPHA * h_ref[...] + f, g_ref[...], b_ref[...])


def _combine_ln(h2d, yg, info, g, b, batch, drop_front):
    T, D = h2d.shape
    nt = T // TM
    ntb = nt // batch
    if drop_front:
        tile = lambda bb, i: bb * ntb + i + 1
        grid = (batch, ntb - 1)
        out_rows = batch * (ntb - 1) * TM
        omap = lambda bb, i: (bb * (ntb - 1) + i, 0)
    else:
        tile = lambda bb, i: bb * ntb + i
        grid = (batch, ntb)
        out_rows = T
        omap = lambda bb, i: (bb * ntb + i, 0)
    return pl.pallas_call(
        _combine_kernel,
        grid=grid,
        in_specs=[
            pl.BlockSpec((TM, D), lambda bb, i: (tile(bb, i), 0)),
            pl.BlockSpec((TM, D), lambda bb, i: (tile(bb, i), 0)),
            pl.BlockSpec((TM, D), lambda bb, i: (nt + tile(bb, i), 0)),
            pl.BlockSpec((TM, LANES), lambda bb, i: (tile(bb, i), 0)),
            pl.BlockSpec((1, D), lambda bb, i: (0, 0)),
            pl.BlockSpec((1, D), lambda bb, i: (0, 0)),
        ],
        out_specs=pl.BlockSpec((TM, D), omap),
        out_shape=jax.ShapeDtypeStruct((out_rows, D), jnp.float32),
        compiler_params=_cparams(("parallel", "parallel")),
        name="moe_combine_ln",
    )(h2d, yg, yg, info, g.reshape(1, D), b.reshape(1, D))


def _moe_layer(h2d, wg, bg, we, be, w1, w3, w2, pg, pb, batch, drop_front):
    T, D = h2d.shape
    n_rows = 2 * T + N_EXPERTS * TG
    info, cnt = _route(h2d, wg, bg, we, be)
    pos, src, tile_expert, n_used = _dispatch_plan(info, cnt, n_rows)
    xs = _gather_rows(h2d, src)
    ys = _gmm(xs, w1.astype(jnp.bfloat16), w3.astype(jnp.bfloat16), w2.astype(jnp.bfloat16),
              tile_expert, n_used)
    yg = _gather_rows(ys, jnp.concatenate([pos[:, 0], pos[:, 1]]))
    return _combine_ln(h2d, yg, info, pg, pb, batch, drop_front)


def _qkv_kernel(h_ref, wq_ref, wk_ref, wv_ref, cosT_ref, sinT_ref, cn_ref, s1_ref, s2_ref,
                q_ref, k_ref, v_ref):
    hb = h_ref[0].astype(jnp.bfloat16)
    half = ROT_DIM // 2
    nt_dims = (((1,), (1,)), ((), ()))
    qT = lax.dot_general(wq_ref[...], hb, nt_dims, preferred_element_type=jnp.float32)
    cosT = cosT_ref[...]
    sinT = sinT_ref[...]
    scale = HEAD_DIM ** -0.5
    for hh in range(N_HEADS):
        parts = []
        for c in range(2):
            r0 = hh * 2 * HEAD_DIM + c * HEAD_DIM
            t1 = qT[r0:r0 + half]
            t2 = qT[r0 + half:r0 + 2 * half]
            parts += [t1 * cosT - t2 * sinT, t1 * sinT + t2 * cosT, qT[r0 + 2 * half:r0 + HEAD_DIM]]
        q_ref[0, hh, 0] = (jnp.concatenate(parts, axis=0) * scale).astype(jnp.bfloat16)
    k = jnp.dot(hb, wk_ref[...], preferred_element_type=jnp.float32)
    cn = cn_ref[...]
    s1 = s1_ref[...]
    s2 = s2_ref[...]
    for hh in range(N_HEADS):
        kb = k[:, hh * LANES:(hh + 1) * LANES]
        kr = kb * cn + pltpu.roll(kb, LANES - half, 1) * s1 + pltpu.roll(kb, half, 1) * s2
        k_ref[0, hh, 0] = kr.astype(jnp.bfloat16)
    vT = lax.dot_general(wv_ref[...], hb, nt_dims, preferred_element_type=jnp.float32)
    for hh in range(N_HEADS):
        v_ref[0, hh, 0] = vT[hh * V_DIM:(hh + 1) * V_DIM].astype(jnp.bfloat16)


def _rope_tables(lp):
    half = ROT_DIM // 2
    pos = jnp.arange(lp, dtype=jnp.float32) - PAD0
    inv = ROPE_THETA ** (-jnp.arange(0, ROT_DIM, 2, dtype=jnp.float32) / ROT_DIM)
    ang = pos[:, None] * inv[None, :]
    cos, sin = jnp.cos(ang), jnp.sin(ang)
    lane = jnp.arange(LANES) % HEAD_DIM
    is1 = lane < half
    is2 = (lane >= half) & (lane < ROT_DIM)
    fi = jnp.where(is1, lane, jnp.where(is2, lane - half, 0))
    cn = jnp.where(is1 | is2, cos[:, fi], 1.0)
    s1 = jnp.where(is1, -sin[:, fi], 0.0)
    s2 = jnp.where(is2, sin[:, fi], 0.0)
    return cos.T, sin.T, cn, s1, s2


def _qkv(h, wq, wk, wv):
    B, lp, D = h.shape
    nt = lp // TM
    cosT, sinT, cn, s1, s2 = _rope_tables(lp)
    full2 = lambda b, i: (0, 0)
    hd2 = 2 * HEAD_DIM
    out5 = lambda r, c: jax.ShapeDtypeStruct((B, N_HEADS, nt, r, c), jnp.bfloat16)
    spec5 = lambda r, c: pl.BlockSpec((1, N_HEADS, 1, r, c), lambda b, i: (b, 0, i, 0, 0))
    return pl.pallas_call(
        _qkv_kernel,
        grid=(B, nt),
        in_specs=[
            pl.BlockSpec((1, TM, D), lambda b, i: (b, i, 0)),
            pl.BlockSpec((D, D), full2),
            pl.BlockSpec((D, D), full2),
            pl.BlockSpec((D, D), full2),
            pl.BlockSpec((ROT_DIM // 2, TM), lambda b, i: (0, i)),
            pl.BlockSpec((ROT_DIM // 2, TM), lambda b, i: (0, i)),
            pl.BlockSpec((TM, LANES), lambda b, i: (i, 0)),
            pl.BlockSpec((TM, LANES), lambda b, i: (i, 0)),
            pl.BlockSpec((TM, LANES), lambda b, i: (i, 0)),
        ],
        out_specs=[spec5(hd2, TM), spec5(TM, hd2), spec5(V_DIM, TM)],
        out_shape=[out5(hd2, TM), out5(TM, hd2), out5(V_DIM, TM)],
        compiler_params=_cparams(("parallel", "parallel")),
        name="qkv_proj",
    )(h, wq.T.astype(jnp.bfloat16), wk.astype(jnp.bfloat16), wv.T.astype(jnp.bfloat16), cosT, sinT, cn, s1, s2)


def _attn_kernel(lam_init, q_ref, k_ref, v_ref, lam_ref, g_ref, o_ref, m_ref, l_ref, acc_ref):
    qi = pl.program_id(2)
    q = q_ref[0, 0, 0]
    zero = jnp.zeros((HEAD_DIM, TM), jnp.bfloat16)
    qm = (jnp.concatenate([q[:HEAD_DIM], zero], axis=0), jnp.concatenate([zero, q[HEAD_DIM:]], axis=0))
    m_ref[...] = jnp.full(m_ref.shape, NEG, jnp.float32)
    l_ref[...] = jnp.zeros_like(l_ref)
    acc_ref[...] = jnp.zeros_like(acc_ref)

    def step(ki, masked):
        kt = k_ref[0, 0, ki]
        vt = v_ref[0, 0, ki]
        if masked:
            kpos = ki * TM + lax.broadcasted_iota(jnp.int32, (TM, TM), 0)
            qpos = qi * TM + lax.broadcasted_iota(jnp.int32, (TM, TM), 1)
            allowed = ((kpos >> 6) <= (qpos >> 6)) & (kpos >= PAD0)
        for c in range(2):
            s = jnp.dot(kt, qm[c], preferred_element_type=jnp.float32)
            if masked:
                s = jnp.where(allowed, s, NEG)
            m_old = m_ref[c:c + 1, :]
            m_new = jnp.maximum(m_old, jnp.max(s, axis=0, keepdims=True))
            alpha = jnp.exp(m_old - m_new)
            p = jnp.exp(s - m_new)
            l_ref[c:c + 1, :] = alpha * l_ref[c:c + 1, :] + jnp.sum(p, axis=0, keepdims=True)
            m_ref[c:c + 1, :] = m_new
            pv = jnp.dot(vt, p.astype(jnp.bfloat16), preferred_element_type=jnp.float32)
            acc_ref[c] = alpha * acc_ref[c] + pv

    @pl.when(qi > 0)
    def _():
        step(0, True)

    def body(ki, carry):
        step(ki, False)
        return carry

    lax.fori_loop(1, qi, body, 0)
    step(qi, True)

    lp = lam_ref[...]
    lam = (jnp.exp(jnp.sum(lp[0:1] * lp[1:2], axis=1, keepdims=True))
           - jnp.exp(jnp.sum(lp[2:3] * lp[3:4], axis=1, keepdims=True)) + lam_init)
    o = acc_ref[0] / l_ref[0:1, :] - lam * (acc_ref[1] / l_ref[1:2, :])
    ms = jnp.mean(o * o, axis=0, keepdims=True)
    y = o * lax.rsqrt(ms + EPS) * g_ref[...] * (1.0 - lam_init)
    o_ref[0] = y.T.astype(o_ref.dtype)


def _attention(q5, k5, v5, lam_params, subln_g, lam_init):
    B, H, nt, _, _ = q5.shape
    lp = nt * TM
    kernel = functools.partial(_attn_kernel, lam_init)
    return pl.pallas_call(
        kernel,
        grid=(B, H, nt),
        in_specs=[
            pl.BlockSpec((1, 1, 1, 2 * HEAD_DIM, TM), lambda b, h, i: (b, h, i, 0, 0)),
            pl.BlockSpec((1, 1, nt, TM, 2 * HEAD_DIM), lambda b, h, i: (b, h, 0, 0, 0)),
            pl.BlockSpec((1, 1, nt, V_DIM, TM), lambda b, h, i: (b, h, 0, 0, 0)),
            pl.BlockSpec((4, HEAD_DIM), lambda b, h, i: (0, 0)),
            pl.BlockSpec((V_DIM, 1), lambda b, h, i: (0, 0)),
        ],
        out_specs=pl.BlockSpec((1, TM, V_DIM), lambda b, h, i: (b, i, h)),
        out_shape=jax.ShapeDtypeStruct((B, lp, H * V_DIM), jnp.bfloat16),
        scratch_shapes=[
            pltpu.VMEM((8, TM), jnp.float32),
            pltpu.VMEM((8, TM), jnp.float32),
            pltpu.VMEM((2, V_DIM, TM), jnp.float32),
        ],
        compiler_params=_cparams(("parallel", "parallel", "arbitrary")),
        name="diff_attention",
    )(q5, k5, v5, lam_params, subln_g.reshape(V_DIM, 1))


def _attn_out_kernel(o_ref, w_ref, h_ref, g_ref, b_ref, out_ref):
    m = jnp.dot(o_ref[0], w_ref[...], preferred_element_type=jnp.float32)
    out_ref[0] = _layernorm(ALPHA * h_ref[0] + m, g_ref[...], b_ref[...])


def _attn_out(o, w_o, h, g, b):
    B, lp, D = h.shape
    nt = lp // TM
    full2 = lambda bb, i: (0, 0)
    tile = pl.BlockSpec((1, TM, D), lambda bb, i: (bb, i, 0))
    return pl.pallas_call(
        _attn_out_kernel,
        grid=(B, nt),
        in_specs=[tile, pl.BlockSpec((D, D), full2), tile, pl.BlockSpec((1, D), full2), pl.BlockSpec((1, D), full2)],
        out_specs=tile,
        out_shape=jax.ShapeDtypeStruct((B, lp, D), jnp.float32),
        compiler_params=_cparams(("parallel", "parallel")),
        name="attn_out_ln",
    )(o, w_o.astype(jnp.bfloat16), h, g.reshape(1, D), b.reshape(1, D))


def kernel(x, meta_tokens, conv_w_pw1, conv_b_pw1, conv_w_dw, conv_b_dw, conv_ln_g, conv_ln_b, conv_w_pw2, conv_b_pw2, kv_w_k, kv_w_v, attn_w_q, attn_lam_q1, attn_lam_k1, attn_lam_q2, attn_lam_k2, attn_subln_g, attn_w_o, post_ln_g, post_ln_b, moe_w_group, moe_b_group, moe_w_expert, moe_b_expert, moe_w1, moe_w3, moe_w2):
    B, S, D = x.shape
    assert D == D_MODEL and S % TM == 0 and S % CHUNK == 0
    assert conv_w_pw1.shape[0] == 1 and attn_w_q.shape[0] == 1 and post_ln_g.shape[0] == DEPTH
    lp = S + TM

    def moe(h, l, drop_front):
        out = _moe_layer(h.reshape(B * lp, D), moe_w_group[l], moe_b_group[l], moe_w_expert[l],
                         moe_b_expert[l], moe_w1[l], moe_w3[l], moe_w2[l],
                         post_ln_g[l, 1], post_ln_b[l, 1], B, drop_front)
        return out.reshape(B, -1, D)

    h = _conv_layer(x, meta_tokens, conv_w_pw1[0], conv_b_pw1[0], conv_w_dw[0], conv_b_dw[0],
                    conv_ln_g[0], conv_ln_b[0], conv_w_pw2[0], conv_b_pw2[0],
                    post_ln_g[0, 0], post_ln_b[0, 0])
    h = moe(h, 0, False)
    q5, k5, v5 = _qkv(h, attn_w_q[0], kv_w_k, kv_w_v)
    lam_init = 0.8 - 0.6 * math.exp(-0.3 * 1)
    lam_params = jnp.stack([attn_lam_q1[0], attn_lam_k1[0], attn_lam_q2[0], attn_lam_k2[0]])
    o = _attention(q5, k5, v5, lam_params, attn_subln_g[0], lam_init)
    h = _attn_out(o, attn_w_o[0], h, post_ln_g[1, 0], post_ln_b[1, 0])
    return moe(h, 1, True)
```

```python
import functools
import math

import jax
import jax.numpy as jnp
from jax import lax
from jax.experimental import pallas as pl
from jax.experimental.pallas import tpu as pltpu

D_MODEL = 1024
N_META = 16
CHUNK = 64
DEPTH = 2
ALPHA = (2.0 * DEPTH) ** 0.25
CONV_W = 31
N_HEADS = 8
HEAD_DIM = 64
V_DIM = 128
ROT_DIM = 16
ROPE_THETA = 500000.0
N_GROUPS = 4
EXPERTS_PER_GROUP = 4
N_EXPERTS = 16
D_EXPERT = 512
EPS = 1e-5

TM = 256
PAD0 = TM - N_META
HALO = 32
TG = 256
LANES = 128
NEG = -1e30
VMEM_LIMIT = 48 * 1024 * 1024


def _cparams(sem):
    return pltpu.CompilerParams(dimension_semantics=sem, vmem_limit_bytes=VMEM_LIMIT)


def _layernorm(x, g, b):
    mu = jnp.mean(x, axis=-1, keepdims=True)
    xc = x - mu
    var = jnp.mean(xc * xc, axis=-1, keepdims=True)
    return xc * lax.rsqrt(var + EPS) * g + b


def _sigmoid(x):
    return 1.0 / (1.0 + jnp.exp(-x))


def _front_tile(meta_ref):
    return jnp.concatenate([jnp.zeros((PAD0, D_MODEL), jnp.float32), meta_ref[...]], axis=0)


def _stream_tile(i, x_ref, meta_ref):
    return jnp.where(i == 0, _front_tile(meta_ref), x_ref[0])


def _conv_in_kernel(x_ref, meta_ref, w_ref, b_ref, u_ref):
    i = pl.program_id(1)
    xt = _stream_tile(i, x_ref, meta_ref)
    h = jnp.dot(xt.astype(jnp.bfloat16), w_ref[...], preferred_element_type=jnp.float32) + b_ref[...]
    u = h[:, :D_MODEL] * _sigmoid(h[:, D_MODEL:])
    row = lax.broadcasted_iota(jnp.int32, (TM, 1), 0)
    u_ref[0] = jnp.where((i == 0) & (row < PAD0), 0.0, u)


def _conv_mid_kernel(ucur_ref, uprev_ref, x_ref, meta_ref, wdw_ref, bdw_ref, lng_ref, lnb_ref,
                     w2_ref, b2_ref, pg_ref, pb_ref, h_ref, win_ref, conv_ref):
    i = pl.program_id(1)
    win_ref[0:HALO, :] = jnp.where(i == 0, 0.0, uprev_ref[0])
    win_ref[HALO:HALO + TM, :] = ucur_ref[0]
    off = HALO - (CONV_W - 1)
    for c in range(D_MODEL // LANES):
        cs = slice(c * LANES, (c + 1) * LANES)
        acc = jnp.zeros((TM, LANES), jnp.float32)
        for j in range(CONV_W):
            acc = acc + win_ref[off + j:off + j + TM, cs] * wdw_ref[j:j + 1, cs]
        conv_ref[:, cs] = acc
    y = conv_ref[...] + bdw_ref[...]
    y = _layernorm(y, lng_ref[...], lnb_ref[...])
    y = y * _sigmoid(y)
    m = jnp.dot(y.astype(jnp.bfloat16), w2_ref[...], preferred_element_type=jnp.float32) + b2_ref[...]
    h0 = _stream_tile(i, x_ref, meta_ref)
    h_ref[0] = _layernorm(ALPHA * h0 + m, pg_ref[...], pb_ref[...])


def _conv_layer(x, meta, w1, b1, wdw, bdw, lng, lnb, w2, b2, pg, pb):
    B, S, D = x.shape
    nt = S // TM + 1
    lp = nt * TM
    xmap = lambda b, i: (b, jnp.maximum(i - 1, 0), 0)
    full2 = lambda b, i: (0, 0)
    u = pl.pallas_call(
        _conv_in_kernel,
        grid=(B, nt),
        in_specs=[
            pl.BlockSpec((1, TM, D), xmap),
            pl.BlockSpec((N_META, D), full2),
            pl.BlockSpec((D, 2 * D), full2),
            pl.BlockSpec((1, 2 * D), full2),
        ],
        out_specs=pl.BlockSpec((1, TM, D), lambda b, i: (b, i, 0)),
        out_shape=jax.ShapeDtypeStruct((B, lp, D), jnp.float32),
        compiler_params=_cparams(("parallel", "parallel")),
        name="conv_in",
    )(x, meta, w1.astype(jnp.bfloat16), b1.reshape(1, -1))
    wdw_p = jnp.concatenate([wdw, jnp.zeros((HALO - CONV_W, D), wdw.dtype)], axis=0)
    vec = lambda a: a.reshape(1, D)
    h1 = pl.pallas_call(
        _conv_mid_kernel,
        grid=(B, nt),
        in_specs=[
            pl.BlockSpec((1, TM, D), lambda b, i: (b, i, 0)),
            pl.BlockSpec((1, HALO, D), lambda b, i: (b, jnp.maximum(i * (TM // HALO) - 1, 0), 0)),
            pl.BlockSpec((1, TM, D), xmap),
            pl.BlockSpec((N_META, D), full2),
            pl.BlockSpec((HALO, D), full2),
            pl.BlockSpec((1, D), full2),
            pl.BlockSpec((1, D), full2),
            pl.BlockSpec((1, D), full2),
            pl.BlockSpec((D, D), full2),
            pl.BlockSpec((1, D), full2),
            pl.BlockSpec((1, D), full2),
            pl.BlockSpec((1, D), full2),
        ],
        out_specs=pl.BlockSpec((1, TM, D), lambda b, i: (b, i, 0)),
        out_shape=jax.ShapeDtypeStruct((B, lp, D), jnp.float32),
        scratch_shapes=[pltpu.VMEM((HALO + TM, D), jnp.float32), pltpu.VMEM((TM, D), jnp.float32)],
        compiler_params=_cparams(("parallel", "parallel")),
        name="conv_mid",
    )(u, u, x, meta, wdw_p, vec(bdw), vec(lng), vec(lnb), w2.astype(jnp.bfloat16), vec(b2), vec(pg), vec(pb))
    return h1


def _route_kernel(h_ref, whi_ref, wlo_ref, b_ref, info_ref, cnt_ref, carry_ref):
    step = pl.program_id(0)

    @pl.when(step == 0)
    def _():
        carry_ref[...] = jnp.zeros_like(carry_ref)

    h = h_ref[...]
    hhi = h.astype(jnp.bfloat16)
    hlo = (h - hhi.astype(jnp.float32)).astype(jnp.bfloat16)
    lg = (jnp.dot(hhi, whi_ref[...], preferred_element_type=jnp.float32)
          + jnp.dot(hlo, whi_ref[...], preferred_element_type=jnp.float32)
          + jnp.dot(hhi, wlo_ref[...], preferred_element_type=jnp.float32)) + b_ref[...]
    lane_i = lax.broadcasted_iota(jnp.int32, (TM, LANES), 1)
    lane = lane_i.astype(jnp.float32)

    def first_argmax(v, vmax):
        return jnp.min(jnp.where(v == vmax, lane, float(LANES)), axis=1, keepdims=True)

    gmask = lane_i < N_GROUPS
    glog = jnp.where(gmask, lg, NEG)
    gmax = jnp.max(glog, axis=1, keepdims=True)
    gidx = first_argmax(glog, gmax)
    gsum = jnp.sum(jnp.where(gmask, jnp.exp(glog - gmax), 0.0), axis=1, keepdims=True)
    g_w = 1.0 / gsum
    elane = lane_i - N_GROUPS
    egrp = (elane >> 2).astype(jnp.float32)
    emask = (elane >= 0) & (elane < N_EXPERTS) & (egrp == gidx)
    elog = jnp.where(emask, lg, NEG)
    e1 = jnp.max(elog, axis=1, keepdims=True)
    i1 = first_argmax(elog, e1)
    elog2 = jnp.where(lane == i1, NEG, elog)
    e2 = jnp.max(elog2, axis=1, keepdims=True)
    i2 = first_argmax(elog2, e2)
    d = jnp.exp(e2 - e1)
    w1 = g_w / (1.0 + d)
    w2 = g_w * d / (1.0 + d)
    x1 = i1 - float(N_GROUPS)
    x2 = i2 - float(N_GROUPS)
    oh1 = jnp.where(lane == x1, 1.0, 0.0)
    oh2 = jnp.where(lane == x2, 1.0, 0.0)
    r_i = lax.broadcasted_iota(jnp.int32, (TM, TM), 0)
    c_i = lax.broadcasted_iota(jnp.int32, (TM, TM), 1)
    ltri = jnp.where(c_i < r_i, 1.0, 0.0).astype(jnp.bfloat16)
    pre1 = jnp.dot(ltri, oh1.astype(jnp.bfloat16), preferred_element_type=jnp.float32)
    pre2 = jnp.dot(ltri, oh2.astype(jnp.bfloat16), preferred_element_type=jnp.float32)
    cnt1 = jnp.sum(oh1, axis=0, keepdims=True)
    cnt2 = jnp.sum(oh2, axis=0, keepdims=True)
    carry = carry_ref[0:1, :]
    rank1 = jnp.sum(oh1 * (carry + pre1), axis=1, keepdims=True)
    rank2 = jnp.sum(oh2 * (carry + cnt1 + pre2), axis=1, keepdims=True)
    new_carry = carry + cnt1 + cnt2
    carry_ref[...] = jnp.broadcast_to(new_carry, carry_ref.shape)
    cnt_ref[...] = jnp.broadcast_to(new_carry, cnt_ref.shape)
    info = jnp.where(lane_i == 0, x1, 0.0)
    info = jnp.where(lane_i == 1, x2, info)
    info = jnp.where(lane_i == 2, w1, info)
    info = jnp.where(lane_i == 3, w2, info)
    info = jnp.where(lane_i == 4, rank1, info)
    info = jnp.where(lane_i == 5, rank2, info)
    info_ref[...] = info


def _route(h2d, w_group, b_group, w_expert, b_expert):
    T, D = h2d.shape
    wr = jnp.zeros((D, LANES), jnp.float32)
    wr = wr.at[:, :N_GROUPS].set(w_group).at[:, N_GROUPS:N_GROUPS + N_EXPERTS].set(w_expert)
    br = jnp.zeros((1, LANES), jnp.float32)
    br = br.at[0, :N_GROUPS].set(b_group).at[0, N_GROUPS:N_GROUPS + N_EXPERTS].set(b_expert)
    whi = wr.astype(jnp.bfloat16)
    wlo = (wr - whi.astype(jnp.float32)).astype(jnp.bfloat16)
    info, cnt = pl.pallas_call(
        _route_kernel,
        grid=(T // TM,),
        in_specs=[
            pl.BlockSpec((TM, D), lambda i: (i, 0)),
            pl.BlockSpec((D, LANES), lambda i: (0, 0)),
            pl.BlockSpec((D, LANES), lambda i: (0, 0)),
            pl.BlockSpec((1, LANES), lambda i: (0, 0)),
        ],
        out_specs=[
            pl.BlockSpec((TM, LANES), lambda i: (i, 0)),
            pl.BlockSpec((8, LANES), lambda i: (0, 0)),
        ],
        out_shape=[
            jax.ShapeDtypeStruct((T, LANES), jnp.float32),
            jax.ShapeDtypeStruct((8, LANES), jnp.float32),
        ],
        scratch_shapes=[pltpu.VMEM((8, LANES), jnp.float32)],
        compiler_params=_cparams(("arbitrary",)),
        name="route",
    )(h2d, whi, wlo, br)
    return info, cnt


def _dispatch_plan(info, cnt, n_rows):
    T = info.shape[0]
    ex = info[:, 0:2].astype(jnp.int32)
    rank = info[:, 4:6].astype(jnp.int32)
    counts = cnt[0, :N_EXPERTS].astype(jnp.int32)
    padded = ((counts + TG - 1) // TG) * TG
    ends = jnp.cumsum(padded)
    starts = ends - padded
    pos = starts[ex] + rank
    tok = jnp.broadcast_to(jnp.arange(T, dtype=jnp.int32)[:, None], (T, 2))
    src = jnp.zeros((n_rows,), jnp.int32).at[pos.reshape(-1)].set(tok.reshape(-1))
    tile_start = jnp.arange(n_rows // TG, dtype=jnp.int32) * TG
    tile_expert = jnp.minimum(
        jnp.sum((tile_start[:, None] >= ends[None, :]).astype(jnp.int32), axis=1), N_EXPERTS - 1)
    n_used = (ends[-1] // TG).astype(jnp.int32).reshape(1)
    return pos, src, tile_expert.astype(jnp.int32), n_used


def _gather_kernel(idx_ref, table_ref, out_ref, sem):
    def row_copy(r):
        return pltpu.make_async_copy(table_ref.at[pl.ds(idx_ref[0, 0, r], 1), :],
                                     out_ref.at[pl.ds(r, 1), :], sem)

    def start(r, c):
        row_copy(2 * r).start(priority=0)
        row_copy(2 * r + 1).start(priority=1)
        return c

    def wait(r, c):
        row_copy(r).wait()
        return c

    lax.fori_loop(0, TG // 2, start, 0, unroll=4)
    lax.fori_loop(0, TG, wait, 0, unroll=8)


def _gather_rows(table, idx):
    n = idx.shape[0]
    D = table.shape[1]
    return pl.pallas_call(
        _gather_kernel,
        grid=(n // TG,),
        in_specs=[
            pl.BlockSpec((1, 1, TG), lambda i: (i, 0, 0), memory_space=pltpu.SMEM),
            pl.BlockSpec(memory_space=pl.ANY),
        ],
        out_specs=pl.BlockSpec((TG, D), lambda i: (i, 0)),
        out_shape=jax.ShapeDtypeStruct((n, D), table.dtype),
        scratch_shapes=[pltpu.SemaphoreType.DMA(())],
        compiler_params=_cparams(("arbitrary",)),
        name="gather_rows",
    )(idx.reshape(n // TG, 1, TG), table)


def _gmm_kernel(te_ref, nu_ref, x_ref, w1_ref, w3_ref, w2_ref, y_ref):
    j = pl.program_id(0)

    @pl.when(j < nu_ref[0])
    def _():
        x = x_ref[...].astype(jnp.bfloat16)
        a = jnp.dot(x, w1_ref[0], preferred_element_type=jnp.float32)
        g = jnp.dot(x, w3_ref[0], preferred_element_type=jnp.float32)
        hh = (a * _sigmoid(a) * g).astype(jnp.bfloat16)
        y_ref[...] = jnp.dot(hh, w2_ref[0], preferred_element_type=jnp.float32)

    @pl.when(j >= nu_ref[0])
    def _():
        y_ref[...] = jnp.zeros_like(y_ref)


def _gmm(xs, w1, w3, w2, tile_expert, n_used):
    n, D = xs.shape
    de = w1.shape[2]
    grid_spec = pltpu.PrefetchScalarGridSpec(
        num_scalar_prefetch=2,
        grid=(n // TG,),
        in_specs=[
            pl.BlockSpec((TG, D), lambda j, te, nu: (j, 0)),
            pl.BlockSpec((1, D, de), lambda j, te, nu: (te[j], 0, 0)),
            pl.BlockSpec((1, D, de), lambda j, te, nu: (te[j], 0, 0)),
            pl.BlockSpec((1, de, D), lambda j, te, nu: (te[j], 0, 0)),
        ],
        out_specs=pl.BlockSpec((TG, D), lambda j, te, nu: (j, 0)),
    )
    return pl.pallas_call(
        _gmm_kernel,
        grid_spec=grid_spec,
        out_shape=jax.ShapeDtypeStruct((n, D), jnp.float32),
        compiler_params=_cparams(("arbitrary",)),
        name="expert_mlp",
    )(tile_expert, n_used, xs, w1, w3, w2)


def _combine_kernel(h_ref, y1_ref, y2_ref, info_ref, g_ref, b_ref, o_ref):
    info = info_ref[...]
    f = info[:, 2:3] * y1_ref[...] + info[:, 3:4] * y2_ref[...]
    o_ref[...] = _layernorm(ALPHA * h_ref[...] + f, g_ref[...], b_ref[...])


def _combine_ln(h2d, yg, info, g, b, batch, drop_front):
    T, D = h2d.shape
    nt = T // TM
    ntb = nt // batch
    if drop_front:
        tile = lambda bb, i: bb * ntb + i + 1
        grid = (batch, ntb - 1)
        out_rows = batch * (ntb - 1) * TM
        omap = lambda bb, i: (bb * (ntb - 1) + i, 0)
    else:
        tile = lambda bb, i: bb * ntb + i
        grid = (batch, ntb)
        out_rows = T
        omap = lambda bb, i: (bb * ntb + i, 0)
    return pl.pallas_call(
        _combine_kernel,
        grid=grid,
        in_specs=[
            pl.BlockSpec((TM, D), lambda bb, i: (tile(bb, i), 0)),
            pl.BlockSpec((TM, D), lambda bb, i: (tile(bb, i), 0)),
            pl.BlockSpec((TM, D), lambda bb, i: (nt + tile(bb, i), 0)),
            pl.BlockSpec((TM, LANES), lambda bb, i: (tile(bb, i), 0)),
            pl.BlockSpec((1, D), lambda bb, i: (0, 0)),
            pl.BlockSpec((1, D), lambda bb, i: (0, 0)),
        ],
        out_specs=pl.BlockSpec((TM, D), omap),
        out_shape=jax.ShapeDtypeStruct((out_rows, D), jnp.float32),
        compiler_params=_cparams(("parallel", "parallel")),
        name="moe_combine_ln",
    )(h2d, yg, yg, info, g.reshape(1, D), b.reshape(1, D))


def _moe_layer(h2d, wg, bg, we, be, w1, w3, w2, pg, pb, batch, drop_front):
    T, D = h2d.shape
    n_rows = 2 * T + N_EXPERTS * TG
    info, cnt = _route(h2d, wg, bg, we, be)
    pos, src, tile_expert, n_used = _dispatch_plan(info, cnt, n_rows)
    xs = _gather_rows(h2d, src)
    ys = _gmm(xs, w1.astype(jnp.bfloat16), w3.astype(jnp.bfloat16), w2.astype(jnp.bfloat16),
              tile_expert, n_used)
    yg = _gather_rows(ys, jnp.concatenate([pos[:, 0], pos[:, 1]]))
    return _combine_ln(h2d, yg, info, pg, pb, batch, drop_front)


def _qkv_kernel(h_ref, wq_ref, wk_ref, wv_ref, cosT_ref, sinT_ref, cn_ref, s1_ref, s2_ref,
                q_ref, k_ref, v_ref):
    hb = h_ref[0].astype(jnp.bfloat16)
    half = ROT_DIM // 2
    nt_dims = (((1,), (1,)), ((), ()))
    qT = lax.dot_general(wq_ref[...], hb, nt_dims, preferred_element_type=jnp.float32)
    cosT = cosT_ref[...]
    sinT = sinT_ref[...]
    scale = HEAD_DIM ** -0.5 * math.log2(math.e)
    for hh in range(N_HEADS):
        parts = []
        for c in range(2):
            r0 = hh * 2 * HEAD_DIM + c * HEAD_DIM
            t1 = qT[r0:r0 + half]
            t2 = qT[r0 + half:r0 + 2 * half]
            parts += [t1 * cosT - t2 * sinT, t1 * sinT + t2 * cosT, qT[r0 + 2 * half:r0 + HEAD_DIM]]
        q_ref[0, hh, 0] = (jnp.concatenate(parts, axis=0) * scale).astype(jnp.bfloat16)
    k = jnp.dot(hb, wk_ref[...], preferred_element_type=jnp.float32)
    cn = cn_ref[...]
    s1 = s1_ref[...]
    s2 = s2_ref[...]
    for hh in range(N_HEADS):
        kb = k[:, hh * LANES:(hh + 1) * LANES]
        kr = kb * cn + pltpu.roll(kb, LANES - half, 1) * s1 + pltpu.roll(kb, half, 1) * s2
        k_ref[0, hh, 0] = kr.astype(jnp.bfloat16)
    vT = lax.dot_general(wv_ref[...], hb, nt_dims, preferred_element_type=jnp.float32)
    extra = lax.broadcasted_iota(jnp.int32, (V_ROWS - V_DIM, TM), 0)
    ones_row = jnp.where(extra == 0, 1.0, 0.0).astype(jnp.bfloat16)
    for hh in range(N_HEADS):
        v_ref[0, hh, 0, 0:V_DIM, :] = vT[hh * V_DIM:(hh + 1) * V_DIM].astype(jnp.bfloat16)
        v_ref[0, hh, 0, V_DIM:V_ROWS, :] = ones_row


def _rope_tables(lp):
    half = ROT_DIM // 2
    pos = jnp.arange(lp, dtype=jnp.float32) - PAD0
    inv = ROPE_THETA ** (-jnp.arange(0, ROT_DIM, 2, dtype=jnp.float32) / ROT_DIM)
    ang = pos[:, None] * inv[None, :]
    cos, sin = jnp.cos(ang), jnp.sin(ang)
    lane = jnp.arange(LANES) % HEAD_DIM
    is1 = lane < half
    is2 = (lane >= half) & (lane < ROT_DIM)
    fi = jnp.where(is1, lane, jnp.where(is2, lane - half, 0))
    cn = jnp.where(is1 | is2, cos[:, fi], 1.0)
    s1 = jnp.where(is1, -sin[:, fi], 0.0)
    s2 = jnp.where(is2, sin[:, fi], 0.0)
    return cos.T, sin.T, cn, s1, s2


def _qkv(h, wq, wk, wv):
    B, lp, D = h.shape
    nt = lp // TM
    cosT, sinT, cn, s1, s2 = _rope_tables(lp)
    full2 = lambda b, i: (0, 0)
    hd2 = 2 * HEAD_DIM
    out5 = lambda r, c: jax.ShapeDtypeStruct((B, N_HEADS, nt, r, c), jnp.bfloat16)
    spec5 = lambda r, c: pl.BlockSpec((1, N_HEADS, 1, r, c), lambda b, i: (b, 0, i, 0, 0))
    return pl.pallas_call(
        _qkv_kernel,
        grid=(B, nt),
        in_specs=[
            pl.BlockSpec((1, TM, D), lambda b, i: (b, i, 0)),
            pl.BlockSpec((D, D), full2),
            pl.BlockSpec((D, D), full2),
            pl.BlockSpec((D, D), full2),
            pl.BlockSpec((ROT_DIM // 2, TM), lambda b, i: (0, i)),
            pl.BlockSpec((ROT_DIM // 2, TM), lambda b, i: (0, i)),
            pl.BlockSpec((TM, LANES), lambda b, i: (i, 0)),
            pl.BlockSpec((TM, LANES), lambda b, i: (i, 0)),
            pl.BlockSpec((TM, LANES), lambda b, i: (i, 0)),
        ],
        out_specs=[spec5(hd2, TM), spec5(TM, hd2), spec5(V_ROWS, TM)],
        out_shape=[out5(hd2, TM), out5(TM, hd2), out5(V_ROWS, TM)],
        compiler_params=_cparams(("parallel", "parallel")),
        name="qkv_proj",
    )(h, wq.T.astype(jnp.bfloat16), wk.astype(jnp.bfloat16), wv.T.astype(jnp.bfloat16), cosT, sinT, cn, s1, s2)


MASK_PAD, MASK_DIAG, MASK_ALL = 1, 2, 4
ATT_UNROLL = 4
SOFTMAX_ROWS = 32


MASK_VALUE = -1e30
V_ROWS = V_DIM + 16
N_CHUNK_T = TM // CHUNK


def _attn_mask_tables():
    krow = jnp.arange(TM)[:, None]
    col = jnp.arange(LANES)[None, :]
    diag = (col < N_CHUNK_T) & (krow // CHUNK == col)
    pad = (col == N_CHUNK_T) & (krow < PAD0)
    everything = jnp.broadcast_to(col == N_CHUNK_T + 1, (TM, LANES))
    none = jnp.zeros((TM, LANES), bool)
    sel = jnp.stack([none, pad, diag, pad | diag, everything]).astype(jnp.bfloat16)
    row = jnp.arange(LANES)[:, None]
    qchunk = (jnp.arange(2 * TM)[None, :] % TM) // CHUNK
    hidden = ((row < N_CHUNK_T) & (row > qchunk)) | (row == N_CHUNK_T) | (row == N_CHUNK_T + 1)
    val = jnp.where(hidden, MASK_VALUE, 0.0).astype(jnp.bfloat16)
    return sel, val


def _attn_kernel(lam_init, q_ref, k_ref, v_ref, sel_ref, val_ref, lam_ref, g_ref, o_ref,
                 qcat_ref, s_ref, mt_ref, p_ref, alpha_ref, m_ref, acc_ref):
    qi = pl.program_id(2)
    W = 2 * TM
    F = 2 * HEAD_DIM
    q = q_ref[0, 0, 0]
    zero = jnp.zeros((HEAD_DIM, TM), jnp.bfloat16)
    qcat_ref[0:F, 0:TM] = jnp.concatenate([q[:HEAD_DIM], zero], axis=0)
    qcat_ref[0:F, TM:W] = jnp.concatenate([zero, q[HEAD_DIM:]], axis=0)
    qcat_ref[F:2 * F, :] = val_ref[...]
    s_ref[1] = jnp.full((TM, W), -jnp.inf, jnp.float32)
    p_ref[0] = jnp.zeros((TM, W), jnp.bfloat16)
    alpha_ref[0] = jnp.ones((8, W), jnp.float32)
    mt_ref[1] = jnp.full((8, W), -jnp.inf, jnp.float32)
    m_ref[...] = jnp.full(m_ref.shape, jnp.finfo(jnp.float32).min, jnp.float32)
    acc_ref[...] = jnp.zeros_like(acc_ref)

    def stages(t, cur):
        prv = 1 - cur
        kind = jnp.where(t > qi, MASK_ALL, jnp.where(t == 0, MASK_PAD, 0) + jnp.where(t == qi, MASK_DIAG, 0))
        lhs = jnp.concatenate([k_ref[0, 0, jnp.minimum(t, qi)], sel_ref[kind]], axis=1)
        s_new = jnp.dot(lhs, qcat_ref[...], preferred_element_type=jnp.float32)
        s_ref[cur] = s_new
        mt_ref[cur, 0:1, :] = jnp.max(s_new, axis=0, keepdims=True)
        vt = v_ref[0, 0, jnp.clip(t - 2, 0, qi)]
        pv = jnp.dot(vt, p_ref[cur], preferred_element_type=jnp.float32)
        acc_ref[...] = alpha_ref[cur, 0:1, :] * acc_ref[...] + pv
        m_old = m_ref[0:1, :]
        m_new = jnp.maximum(m_old, mt_ref[prv, 0:1, :])
        for r in range(0, TM, SOFTMAX_ROWS):
            p = jnp.exp2(s_ref[prv, r:r + SOFTMAX_ROWS, :] - m_new)
            p_ref[prv, r:r + SOFTMAX_ROWS, :] = p.astype(jnp.bfloat16)
        m_ref[0:1, :] = m_new
        alpha_ref[prv, 0:1, :] = jnp.exp2(m_old - m_new)

    def body(i, carry):
        for u in range(ATT_UNROLL):
            stages(i * ATT_UNROLL + u, u & 1)
        return carry

    lax.fori_loop(0, (qi + 2 + ATT_UNROLL) // ATT_UNROLL, body, 0)

    lp = lam_ref[...]
    lam = (jnp.exp(jnp.sum(lp[0:1] * lp[1:2], axis=1, keepdims=True))
           - jnp.exp(jnp.sum(lp[2:3] * lp[3:4], axis=1, keepdims=True)) + lam_init)
    l = acc_ref[V_DIM:V_DIM + 1, :]
    l = jnp.where(l == 0.0, 1.0, l)
    acc = acc_ref[0:V_DIM, :]
    o = acc[:, 0:TM] / l[:, 0:TM] - lam * (acc[:, TM:W] / l[:, TM:W])
    ms = jnp.mean(o * o, axis=0, keepdims=True)
    y = o * lax.rsqrt(ms + EPS) * g_ref[...] * (1.0 - lam_init)
    o_ref[0] = y.T.astype(o_ref.dtype)


def _attention(q5, k5, v5, lam_params, subln_g, lam_init):
    B, H, nt, _, _ = q5.shape
    lp = nt * TM
    kernel = functools.partial(_attn_kernel, lam_init)
    sel, val = _attn_mask_tables()
    return pl.pallas_call(
        kernel,
        grid=(B, H, nt),
        in_specs=[
            pl.BlockSpec((1, 1, 1, 2 * HEAD_DIM, TM), lambda b, h, i: (b, h, i, 0, 0)),
            pl.BlockSpec((1, 1, nt, TM, 2 * HEAD_DIM), lambda b, h, i: (b, h, 0, 0, 0)),
            pl.BlockSpec((1, 1, nt, V_ROWS, TM), lambda b, h, i: (b, h, 0, 0, 0)),
            pl.BlockSpec((5, TM, LANES), lambda b, h, i: (0, 0, 0)),
            pl.BlockSpec((LANES, 2 * TM), lambda b, h, i: (0, 0)),
            pl.BlockSpec((4, HEAD_DIM), lambda b, h, i: (0, 0)),
            pl.BlockSpec((V_DIM, 1), lambda b, h, i: (0, 0)),
        ],
        out_specs=pl.BlockSpec((1, TM, V_DIM), lambda b, h, i: (b, i, h)),
        out_shape=jax.ShapeDtypeStruct((B, lp, H * V_DIM), jnp.bfloat16),
        scratch_shapes=[
            pltpu.VMEM((4 * HEAD_DIM, 2 * TM), jnp.bfloat16),
            pltpu.VMEM((2, TM, 2 * TM), jnp.float32),
            pltpu.VMEM((2, 8, 2 * TM), jnp.float32),
            pltpu.VMEM((2, TM, 2 * TM), jnp.bfloat16),
            pltpu.VMEM((2, 8, 2 * TM), jnp.float32),
            pltpu.VMEM((8, 2 * TM), jnp.float32),
            pltpu.VMEM((V_ROWS, 2 * TM), jnp.float32),
        ],
        compiler_params=_cparams(("parallel", "parallel", "arbitrary")),
        name="diff_attention",
    )(q5, k5, v5, sel, val, lam_params, subln_g.reshape(V_DIM, 1))


def _attn_out_kernel(o_ref, w_ref, h_ref, g_ref, b_ref, out_ref):
    m = jnp.dot(o_ref[0], w_ref[...], preferred_element_type=jnp.float32)
    out_ref[0] = _layernorm(ALPHA * h_ref[0] + m, g_ref[...], b_ref[...])


def _attn_out(o, w_o, h, g, b):
    B, lp, D = h.shape
    nt = lp // TM
    full2 = lambda bb, i: (0, 0)
    tile = pl.BlockSpec((1, TM, D), lambda bb, i: (bb, i, 0))
    return pl.pallas_call(
        _attn_out_kernel,
        grid=(B, nt),
        in_specs=[tile, pl.BlockSpec((D, D), full2), tile, pl.BlockSpec((1, D), full2), pl.BlockSpec((1, D), full2)],
        out_specs=tile,
        out_shape=jax.ShapeDtypeStruct((B, lp, D), jnp.float32),
        compiler_params=_cparams(("parallel", "parallel")),
        name="attn_out_ln",
    )(o, w_o.astype(jnp.bfloat16), h, g.reshape(1, D), b.reshape(1, D))


def kernel(x, meta_tokens, conv_w_pw1, conv_b_pw1, conv_w_dw, conv_b_dw, conv_ln_g, conv_ln_b, conv_w_pw2, conv_b_pw2, kv_w_k, kv_w_v, attn_w_q, attn_lam_q1, attn_lam_k1, attn_lam_q2, attn_lam_k2, attn_subln_g, attn_w_o, post_ln_g, post_ln_b, moe_w_group, moe_b_group, moe_w_expert, moe_b_expert, moe_w1, moe_w3, moe_w2):
    B, S, D = x.shape
    assert D == D_MODEL and S % TM == 0 and S % CHUNK == 0
    assert conv_w_pw1.shape[0] == 1 and attn_w_q.shape[0] == 1 and post_ln_g.shape[0] == DEPTH
    lp = S + TM

    def moe(h, l, drop_front):
        out = _moe_layer(h.reshape(B * lp, D), moe_w_group[l], moe_b_group[l], moe_w_expert[l],
                         moe_b_expert[l], moe_w1[l], moe_w3[l], moe_w2[l],
                         post_ln_g[l, 1], post_ln_b[l, 1], B, drop_front)
        return out.reshape(B, -1, D)

    h = _conv_layer(x, meta_tokens, conv_w_pw1[0], conv_b_pw1[0], conv_w_dw[0], conv_b_dw[0],
                    conv_ln_g[0], conv_ln_b[0], conv_w_pw2[0], conv_b_pw2[0],
                    post_ln_g[0, 0], post_ln_b[0, 0])
    h = moe(h, 0, False)
    q5, k5, v5 = _qkv(h, attn_w_q[0], kv_w_k, kv_w_v)
    lam_init = 0.8 - 0.6 * math.exp(-0.3 * 1)
    lam_params = jnp.stack([attn_lam_q1[0], attn_lam_k1[0], attn_lam_q2[0], attn_lam_k2[0]])
    o = _attention(q5, k5, v5, lam_params, attn_subln_g[0], lam_init)
    h = _attn_out(o, attn_w_o[0], h, post_ln_g[1, 0], post_ln_b[1, 0])
    return moe(h, 1, True)
```

```python
import functools
import math

import jax
import jax.numpy as jnp
from jax import lax
from jax.experimental import pallas as pl
from jax.experimental.pallas import tpu as pltpu

D_MODEL = 1024
N_META = 16
CHUNK = 64
DEPTH = 2
ALPHA = (2.0 * DEPTH) ** 0.25
CONV_W = 31
N_HEADS = 8
HEAD_DIM = 64
V_DIM = 128
ROT_DIM = 16
ROPE_THETA = 500000.0
N_GROUPS = 4
EXPERTS_PER_GROUP = 4
N_EXPERTS = 16
D_EXPERT = 512
EPS = 1e-5

TM = 256
PAD0 = TM - N_META
HALO = 32
CONV_SLACK = 8
TG = 256
ROW_TILE = 8
MOVE_ROWS = 512
LANES = 128
NEG = -1e30
VMEM_LIMIT = 48 * 1024 * 1024


def _cparams(sem):
    return pltpu.CompilerParams(dimension_semantics=sem, vmem_limit_bytes=VMEM_LIMIT)


def _layernorm(x, g, b):
    mu = jnp.mean(x, axis=-1, keepdims=True)
    xc = x - mu
    var = jnp.mean(xc * xc, axis=-1, keepdims=True)
    return xc * lax.rsqrt(var + EPS) * g + b


def _sigmoid(x):
    return 1.0 / (1.0 + jnp.exp(-x))


def _front_tile(meta_ref):
    return jnp.concatenate([jnp.zeros((PAD0, D_MODEL), jnp.float32), meta_ref[...]], axis=0)


def _stream_tile(i, x_ref, meta_ref):
    return jnp.where(i == 0, _front_tile(meta_ref), x_ref[0])


def _conv_in_kernel(x_ref, meta_ref, w_ref, b_ref, u_ref):
    i = pl.program_id(1)
    xt = _stream_tile(i, x_ref, meta_ref)
    h = jnp.dot(xt.astype(jnp.bfloat16), w_ref[...], preferred_element_type=jnp.float32) + b_ref[...]
    u = h[:, :D_MODEL] * _sigmoid(h[:, D_MODEL:])
    row = lax.broadcasted_iota(jnp.int32, (TM, 1), 0)
    u_ref[0] = jnp.where((i == 0) & (row < PAD0), 0.0, u)


def _conv_mid_kernel(ucur_ref, uprev_ref, x_ref, meta_ref, wdw_ref, bdw_ref, lng_ref, lnb_ref,
                     w2_ref, b2_ref, pg_ref, pb_ref, h_ref, win_ref, part_ref, conv_ref):
    i = pl.program_id(1)
    win_ref[0:HALO, :] = jnp.where(i == 0, 0.0, uprev_ref[0])
    win_ref[HALO:HALO + TM, :] = ucur_ref[0]
    win_ref[HALO + TM:, :] = jnp.zeros((CONV_SLACK, D_MODEL), jnp.float32)
    off = HALO - (CONV_W - 1)
    rows = TM + 16
    for c in range(D_MODEL // LANES):
        cs = slice(c * LANES, (c + 1) * LANES)
        acc = jnp.zeros((TM, LANES), jnp.float32)
        for r in range(8):
            part = None
            for j in range(r, CONV_W, 8):
                term = win_ref[j - r:j - r + rows, cs] * wdw_ref[j:j + 1, cs]
                part = term if part is None else part + term
            slot = (c * 8 + r) % 2
            part_ref[slot] = part
            acc = acc + part_ref[slot, off + r:off + r + TM, :]
        conv_ref[:, cs] = acc
    y = conv_ref[...] + bdw_ref[...]
    y = _layernorm(y, lng_ref[...], lnb_ref[...])
    y = y * _sigmoid(y)
    m = jnp.dot(y.astype(jnp.bfloat16), w2_ref[...], preferred_element_type=jnp.float32) + b2_ref[...]
    h0 = _stream_tile(i, x_ref, meta_ref)
    h_ref[0] = _layernorm(ALPHA * h0 + m, pg_ref[...], pb_ref[...])


def _conv_layer(x, meta, w1, b1, wdw, bdw, lng, lnb, w2, b2, pg, pb):
    B, S, D = x.shape
    nt = S // TM + 1
    lp = nt * TM
    xmap = lambda b, i: (b, jnp.maximum(i - 1, 0), 0)
    full2 = lambda b, i: (0, 0)
    u = pl.pallas_call(
        _conv_in_kernel,
        grid=(B, nt),
        in_specs=[
            pl.BlockSpec((1, TM, D), xmap),
            pl.BlockSpec((N_META, D), full2),
            pl.BlockSpec((D, 2 * D), full2),
            pl.BlockSpec((1, 2 * D), full2),
        ],
        out_specs=pl.BlockSpec((1, TM, D), lambda b, i: (b, i, 0)),
        out_shape=jax.ShapeDtypeStruct((B, lp, D), jnp.float32),
        compiler_params=_cparams(("parallel", "parallel")),
        name="conv_in",
    )(x, meta, w1.astype(jnp.bfloat16), b1.reshape(1, -1))
    wdw_p = jnp.concatenate([wdw, jnp.zeros((HALO - CONV_W, D), wdw.dtype)], axis=0)
    vec = lambda a: a.reshape(1, D)
    h1 = pl.pallas_call(
        _conv_mid_kernel,
        grid=(B, nt),
        in_specs=[
            pl.BlockSpec((1, TM, D), lambda b, i: (b, i, 0)),
            pl.BlockSpec((1, HALO, D), lambda b, i: (b, jnp.maximum(i * (TM // HALO) - 1, 0), 0)),
            pl.BlockSpec((1, TM, D), xmap),
            pl.BlockSpec((N_META, D), full2),
            pl.BlockSpec((HALO, D), full2),
            pl.BlockSpec((1, D), full2),
            pl.BlockSpec((1, D), full2),
            pl.BlockSpec((1, D), full2),
            pl.BlockSpec((D, D), full2),
            pl.BlockSpec((1, D), full2),
            pl.BlockSpec((1, D), full2),
            pl.BlockSpec((1, D), full2),
        ],
        out_specs=pl.BlockSpec((1, TM, D), lambda b, i: (b, i, 0)),
        out_shape=jax.ShapeDtypeStruct((B, lp, D), jnp.float32),
        scratch_shapes=[
            pltpu.VMEM((HALO + TM + CONV_SLACK, D), jnp.float32),
            pltpu.VMEM((2, TM + 16, LANES), jnp.float32),
            pltpu.VMEM((TM, D), jnp.float32),
        ],
        compiler_params=_cparams(("parallel", "parallel")),
        name="conv_mid",
    )(u, u, x, meta, wdw_p, vec(bdw), vec(lng), vec(lnb), w2.astype(jnp.bfloat16), vec(b2), vec(pg), vec(pb))
    return h1


def _route_kernel(h_ref, whi_ref, wlo_ref, b_ref, info_ref, cnt_ref, htile_ref, carry_ref):
    step = pl.program_id(0)

    @pl.when(step == 0)
    def _():
        carry_ref[...] = jnp.zeros_like(carry_ref)

    h = h_ref[...]
    _to_token_tiles(htile_ref, h)
    hhi = h.astype(jnp.bfloat16)
    hlo = (h - hhi.astype(jnp.float32)).astype(jnp.bfloat16)
    lg = (jnp.dot(hhi, whi_ref[...], preferred_element_type=jnp.float32)
          + jnp.dot(hlo, whi_ref[...], preferred_element_type=jnp.float32)
          + jnp.dot(hhi, wlo_ref[...], preferred_element_type=jnp.float32)) + b_ref[...]
    lane_i = lax.broadcasted_iota(jnp.int32, (TM, LANES), 1)
    lane = lane_i.astype(jnp.float32)

    def first_argmax(v, vmax):
        return jnp.min(jnp.where(v == vmax, lane, float(LANES)), axis=1, keepdims=True)

    gmask = lane_i < N_GROUPS
    glog = jnp.where(gmask, lg, NEG)
    gmax = jnp.max(glog, axis=1, keepdims=True)
    gidx = first_argmax(glog, gmax)
    gsum = jnp.sum(jnp.where(gmask, jnp.exp(glog - gmax), 0.0), axis=1, keepdims=True)
    g_w = 1.0 / gsum
    elane = lane_i - N_GROUPS
    egrp = (elane >> 2).astype(jnp.float32)
    emask = (elane >= 0) & (elane < N_EXPERTS) & (egrp == gidx)
    elog = jnp.where(emask, lg, NEG)
    e1 = jnp.max(elog, axis=1, keepdims=True)
    i1 = first_argmax(elog, e1)
    elog2 = jnp.where(lane == i1, NEG, elog)
    e2 = jnp.max(elog2, axis=1, keepdims=True)
    i2 = first_argmax(elog2, e2)
    d = jnp.exp(e2 - e1)
    w1 = g_w / (1.0 + d)
    w2 = g_w * d / (1.0 + d)
    x1 = i1 - float(N_GROUPS)
    x2 = i2 - float(N_GROUPS)
    oh1 = jnp.where(lane == x1, 1.0, 0.0)
    oh2 = jnp.where(lane == x2, 1.0, 0.0)
    r_i = lax.broadcasted_iota(jnp.int32, (TM, TM), 0)
    c_i = lax.broadcasted_iota(jnp.int32, (TM, TM), 1)
    ltri = jnp.where(c_i < r_i, 1.0, 0.0).astype(jnp.bfloat16)
    pre1 = jnp.dot(ltri, oh1.astype(jnp.bfloat16), preferred_element_type=jnp.float32)
    pre2 = jnp.dot(ltri, oh2.astype(jnp.bfloat16), preferred_element_type=jnp.float32)
    cnt1 = jnp.sum(oh1, axis=0, keepdims=True)
    cnt2 = jnp.sum(oh2, axis=0, keepdims=True)
    carry = carry_ref[0:1, :]
    rank1 = jnp.sum(oh1 * (carry + pre1), axis=1, keepdims=True)
    rank2 = jnp.sum(oh2 * (carry + cnt1 + pre2), axis=1, keepdims=True)
    new_carry = carry + cnt1 + cnt2
    carry_ref[...] = jnp.broadcast_to(new_carry, carry_ref.shape)
    cnt_ref[...] = jnp.broadcast_to(new_carry, cnt_ref.shape)
    info = jnp.where(lane_i == 0, x1, 0.0)
    info = jnp.where(lane_i == 1, x2, info)
    info = jnp.where(lane_i == 2, w1, info)
    info = jnp.where(lane_i == 3, w2, info)
    info = jnp.where(lane_i == 4, rank1, info)
    info = jnp.where(lane_i == 5, rank2, info)
    info_ref[...] = info


def _route(h2d, w_group, b_group, w_expert, b_expert):
    T, D = h2d.shape
    wr = jnp.zeros((D, LANES), jnp.float32)
    wr = wr.at[:, :N_GROUPS].set(w_group).at[:, N_GROUPS:N_GROUPS + N_EXPERTS].set(w_expert)
    br = jnp.zeros((1, LANES), jnp.float32)
    br = br.at[0, :N_GROUPS].set(b_group).at[0, N_GROUPS:N_GROUPS + N_EXPERTS].set(b_expert)
    whi = wr.astype(jnp.bfloat16)
    wlo = (wr - whi.astype(jnp.float32)).astype(jnp.bfloat16)
    return pl.pallas_call(
        _route_kernel,
        grid=(T // TM,),
        in_specs=[
            pl.BlockSpec((TM, D), lambda i: (i, 0)),
            pl.BlockSpec((D, LANES), lambda i: (0, 0)),
            pl.BlockSpec((D, LANES), lambda i: (0, 0)),
            pl.BlockSpec((1, LANES), lambda i: (0, 0)),
        ],
        out_specs=[
            pl.BlockSpec((TM, LANES), lambda i: (i, 0)),
            pl.BlockSpec((8, LANES), lambda i: (0, 0)),
            pl.BlockSpec((TM * ROW_TILE, LANES), lambda i: (i, 0)),
        ],
        out_shape=[
            jax.ShapeDtypeStruct((T, LANES), jnp.float32),
            jax.ShapeDtypeStruct((8, LANES), jnp.float32),
            jax.ShapeDtypeStruct((T * ROW_TILE, LANES), jnp.float32),
        ],
        scratch_shapes=[pltpu.VMEM((8, LANES), jnp.float32)],
        compiler_params=_cparams(("arbitrary",)),
        name="route",
    )(h2d, whi, wlo, br)


def _dispatch_plan(info, cnt, n_rows):
    ex = info[:, 0:2].astype(jnp.int32)
    rank = info[:, 4:6].astype(jnp.int32)
    counts = cnt[0, :N_EXPERTS].astype(jnp.int32)
    padded = ((counts + TG - 1) // TG) * TG
    ends = jnp.cumsum(padded)
    starts = ends - padded
    pos = starts[ex] + rank
    i = jnp.arange(TG, dtype=jnp.int32)[None, :]
    pad_rows = jnp.where(i < (padded - counts)[:, None], (starts + counts)[:, None] + i, -1)
    tile_start = jnp.arange(n_rows // TG, dtype=jnp.int32) * TG
    tile_expert = jnp.minimum(
        jnp.sum((tile_start[:, None] >= ends[None, :]).astype(jnp.int32), axis=1), N_EXPERTS - 1)
    n_used = (ends[-1] // TG).astype(jnp.int32).reshape(1)
    return pos, pad_rows, tile_expert.astype(jnp.int32), n_used


def _to_token_tiles(ref, x):
    rows = x.shape[0]
    for j in range(ROW_TILE):
        ref[pl.ds(j, rows, stride=ROW_TILE), :] = x[:, j * LANES:(j + 1) * LANES]


def _from_token_tiles(ref, rows):
    return jnp.concatenate([ref[pl.ds(j, rows, stride=ROW_TILE), :] for j in range(ROW_TILE)], axis=1)


def _move_kernel(src_ref, dst_ref, table_ref, out_ref, sem):
    def row_copy(r):
        return pltpu.make_async_copy(table_ref.at[src_ref[0, 0, r]], out_ref.at[dst_ref[0, 0, r]], sem)

    def start(r, c):
        row_copy(2 * r).start(priority=0)
        row_copy(2 * r + 1).start(priority=1)
        return c

    def wait(r, c):
        row_copy(r).wait()
        return c

    lax.fori_loop(0, MOVE_ROWS // 2, start, 0, unroll=4)
    lax.fori_loop(0, MOVE_ROWS, wait, 0, unroll=8)


def _move_rows(table, src, dst, n_out):
    n = src.shape[0]
    idx_spec = pl.BlockSpec((1, 1, MOVE_ROWS), lambda i: (i, 0, 0), memory_space=pltpu.SMEM)
    return pl.pallas_call(
        _move_kernel,
        grid=(n // MOVE_ROWS,),
        in_specs=[idx_spec, idx_spec, pl.BlockSpec(memory_space=pl.ANY)],
        out_specs=pl.BlockSpec(memory_space=pl.ANY),
        out_shape=jax.ShapeDtypeStruct((n_out,) + table.shape[1:], table.dtype),
        scratch_shapes=[pltpu.SemaphoreType.DMA(())],
        compiler_params=_cparams(("arbitrary",)),
        name="move_rows",
    )(src.reshape(n // MOVE_ROWS, 1, MOVE_ROWS), dst.reshape(n // MOVE_ROWS, 1, MOVE_ROWS), table)


def _dispatch_kernel(n_tiles, pads_per_step, tails_per_step, src_ref, dst_ref, pad_ref, nu_ref, table_ref,
                     out_ref, zrow_ref, ztile_ref, sem, zsem):
    step = pl.program_id(0)

    @pl.when(step == 0)
    def _():
        zrow_ref[...] = jnp.zeros_like(zrow_ref)
        ztile_ref[...] = jnp.zeros_like(ztile_ref)

    _move_kernel(src_ref, dst_ref, table_ref, out_ref, sem)

    def tail_copy(k):
        tail = nu_ref[0] + step * tails_per_step + k
        copy = pltpu.make_async_copy(ztile_ref, out_ref.at[pl.ds(jnp.minimum(tail, n_tiles - 1) * TG, TG)], zsem)
        return tail < n_tiles, copy

    def pad_copy(r):
        return pltpu.make_async_copy(zrow_ref, out_ref.at[jnp.maximum(pad_ref[0, 0, r], 0)], zsem)

    def start(r, c):
        @pl.when(pad_ref[0, 0, r] >= 0)
        def _():
            pad_copy(r).start()
        return c

    def wait(r, c):
        @pl.when(pad_ref[0, 0, r] >= 0)
        def _():
            pad_copy(r).wait()
        return c

    for k in range(tails_per_step):
        live, copy = tail_copy(k)
        pl.when(live)(copy.start)
    lax.fori_loop(0, pads_per_step, start, 0)
    lax.fori_loop(0, pads_per_step, wait, 0)
    for k in range(tails_per_step):
        live, copy = tail_copy(k)
        pl.when(live)(copy.wait)


def _dispatch_rows(table, src, dst, pad_rows, n_used, n_out):
    n = src.shape[0]
    steps = n // MOVE_ROWS
    pads_per_step = pl.cdiv(pad_rows.size, steps)
    tails_per_step = pl.cdiv(n_out // TG - n // TG, steps)
    pads = jnp.concatenate([pad_rows.reshape(-1),
                            jnp.full((steps * pads_per_step - pad_rows.size,), -1, jnp.int32)])
    idx_spec = pl.BlockSpec((1, 1, MOVE_ROWS), lambda i: (i, 0, 0), memory_space=pltpu.SMEM)
    return pl.pallas_call(
        functools.partial(_dispatch_kernel, n_out // TG, pads_per_step, tails_per_step),
        grid=(steps,),
        in_specs=[
            idx_spec, idx_spec,
            pl.BlockSpec((1, 1, pads_per_step), lambda i: (i, 0, 0), memory_space=pltpu.SMEM),
            pl.BlockSpec(memory_space=pltpu.SMEM),
            pl.BlockSpec(memory_space=pl.ANY),
        ],
        out_specs=pl.BlockSpec(memory_space=pl.ANY),
        out_shape=jax.ShapeDtypeStruct((n_out,) + table.shape[1:], table.dtype),
        scratch_shapes=[
            pltpu.VMEM(table.shape[1:], table.dtype),
            pltpu.VMEM((TG,) + table.shape[1:], table.dtype),
            pltpu.SemaphoreType.DMA(()),
            pltpu.SemaphoreType.DMA(()),
        ],
        compiler_params=_cparams(("arbitrary",)),
        name="dispatch_rows",
    )(src.reshape(steps, 1, MOVE_ROWS), dst.reshape(steps, 1, MOVE_ROWS),
      pads.reshape(steps, 1, pads_per_step), n_used, table)


def _gmm_kernel(te_ref, nu_ref, x_ref, w1_ref, w3_ref, w2_ref, y_ref):
    j = pl.program_id(0)

    @pl.when(j < nu_ref[0])
    def _():
        x = _from_token_tiles(x_ref, TG).astype(jnp.bfloat16)
        a = jnp.dot(x, w1_ref[0], preferred_element_type=jnp.float32)
        g = jnp.dot(x, w3_ref[0], preferred_element_type=jnp.float32)
        hh = (a * _sigmoid(a) * g).astype(jnp.bfloat16)
        _to_token_tiles(y_ref, jnp.dot(hh, w2_ref[0], preferred_element_type=jnp.float32))

    @pl.when(j >= nu_ref[0])
    def _():
        y_ref[...] = jnp.zeros_like(y_ref)


def _gmm(xs2d, n_rows, w1, w3, w2, tile_expert, n_used):
    D, de = w1.shape[1], w1.shape[2]
    blk = pl.BlockSpec((TG * ROW_TILE, LANES), lambda j, te, nu: (j, 0))
    grid_spec = pltpu.PrefetchScalarGridSpec(
        num_scalar_prefetch=2,
        grid=(n_rows // TG,),
        in_specs=[
            blk,
            pl.BlockSpec((1, D, de), lambda j, te, nu: (te[j], 0, 0)),
            pl.BlockSpec((1, D, de), lambda j, te, nu: (te[j], 0, 0)),
            pl.BlockSpec((1, de, D), lambda j, te, nu: (te[j], 0, 0)),
        ],
        out_specs=blk,
    )
    return pl.pallas_call(
        _gmm_kernel,
        grid_spec=grid_spec,
        out_shape=jax.ShapeDtypeStruct((n_rows * ROW_TILE, LANES), jnp.float32),
        compiler_params=_cparams(("arbitrary",)),
        name="expert_mlp",
    )(tile_expert, n_used, xs2d, w1, w3, w2)


def _combine_kernel(h_ref, y1_ref, y2_ref, info_ref, g_ref, b_ref, o_ref):
    info = info_ref[...]
    f = info[:, 2:3] * _from_token_tiles(y1_ref, TM) + info[:, 3:4] * _from_token_tiles(y2_ref, TM)
    o_ref[...] = _layernorm(ALPHA * h_ref[...] + f, g_ref[...], b_ref[...])


def _combine_ln(h2d, yg2d, info, g, b, batch, drop_front):
    T, D = h2d.shape
    nt = T // TM
    ntb = nt // batch
    if drop_front:
        tile = lambda bb, i: bb * ntb + i + 1
        grid = (batch, ntb - 1)
        out_rows = batch * (ntb - 1) * TM
        omap = lambda bb, i: (bb * (ntb - 1) + i, 0)
    else:
        tile = lambda bb, i: bb * ntb + i
        grid = (batch, ntb)
        out_rows = T
        omap = lambda bb, i: (bb * ntb + i, 0)
    return pl.pallas_call(
        _combine_kernel,
        grid=grid,
        in_specs=[
            pl.BlockSpec((TM, D), lambda bb, i: (tile(bb, i), 0)),
            pl.BlockSpec((TM * ROW_TILE, LANES), lambda bb, i: (tile(bb, i), 0)),
            pl.BlockSpec((TM * ROW_TILE, LANES), lambda bb, i: (nt + tile(bb, i), 0)),
            pl.BlockSpec((TM, LANES), lambda bb, i: (tile(bb, i), 0)),
            pl.BlockSpec((1, D), lambda bb, i: (0, 0)),
            pl.BlockSpec((1, D), lambda bb, i: (0, 0)),
        ],
        out_specs=pl.BlockSpec((TM, D), omap),
        out_shape=jax.ShapeDtypeStruct((out_rows, D), jnp.float32),
        compiler_params=_cparams(("parallel", "parallel")),
        name="moe_combine_ln",
    )(h2d, yg2d, yg2d, info, g.reshape(1, D), b.reshape(1, D))


def _moe_layer(h2d, wg, bg, we, be, w1, w3, w2, pg, pb, batch, drop_front):
    T, D = h2d.shape
    n_rows = 2 * T + N_EXPERTS * TG
    info, cnt, h_tiles = _route(h2d, wg, bg, we, be)
    pos, pad_rows, tile_expert, n_used = _dispatch_plan(info, cnt, n_rows)
    tok = jnp.arange(2 * T, dtype=jnp.int32)
    xs = _dispatch_rows(h_tiles.reshape(T, ROW_TILE, LANES), tok // 2, pos.reshape(-1), pad_rows, n_used, n_rows)
    ys = _gmm(xs.reshape(-1, LANES), n_rows, w1.astype(jnp.bfloat16), w3.astype(jnp.bfloat16),
              w2.astype(jnp.bfloat16), tile_expert, n_used)
    yg = _move_rows(ys.reshape(n_rows, ROW_TILE, LANES), jnp.concatenate([pos[:, 0], pos[:, 1]]), tok, 2 * T)
    return _combine_ln(h2d, yg.reshape(-1, LANES), info, pg, pb, batch, drop_front)


def _qkv_kernel(h_ref, wq_ref, wk_ref, wv_ref, cosT_ref, sinT_ref, cn_ref, s1_ref, s2_ref,
                q_ref, k_ref, v_ref):
    hb = h_ref[0].astype(jnp.bfloat16)
    half = ROT_DIM // 2
    nt_dims = (((1,), (1,)), ((), ()))
    qT = lax.dot_general(wq_ref[...], hb, nt_dims, preferred_element_type=jnp.float32)
    cosT = cosT_ref[...]
    sinT = sinT_ref[...]
    scale = HEAD_DIM ** -0.5 * math.log2(math.e)
    for hh in range(N_HEADS):
        parts = []
        for c in range(2):
            r0 = hh * 2 * HEAD_DIM + c * HEAD_DIM
            t1 = qT[r0:r0 + half]
            t2 = qT[r0 + half:r0 + 2 * half]
            parts += [t1 * cosT - t2 * sinT, t1 * sinT + t2 * cosT, qT[r0 + 2 * half:r0 + HEAD_DIM]]
        q_ref[0, hh, 0] = (jnp.concatenate(parts, axis=0) * scale).astype(jnp.bfloat16)
    k = jnp.dot(hb, wk_ref[...], preferred_element_type=jnp.float32)
    cn = cn_ref[...]
    s1 = s1_ref[...]
    s2 = s2_ref[...]
    for hh in range(N_HEADS):
        kb = k[:, hh * LANES:(hh + 1) * LANES]
        kr = kb * cn + pltpu.roll(kb, LANES - half, 1) * s1 + pltpu.roll(kb, half, 1) * s2
        k_ref[0, hh, 0] = kr.astype(jnp.bfloat16)
    vT = lax.dot_general(wv_ref[...], hb, nt_dims, preferred_element_type=jnp.float32)
    extra = lax.broadcasted_iota(jnp.int32, (V_ROWS - V_DIM, TM), 0)
    ones_row = jnp.where(extra == 0, 1.0, 0.0).astype(jnp.bfloat16)
    for hh in range(N_HEADS):
        v_ref[0, hh, 0, 0:V_DIM, :] = vT[hh * V_DIM:(hh + 1) * V_DIM].astype(jnp.bfloat16)
        v_ref[0, hh, 0, V_DIM:V_ROWS, :] = ones_row


def _rope_tables(lp):
    half = ROT_DIM // 2
    pos = jnp.arange(lp, dtype=jnp.float32) - PAD0
    inv = ROPE_THETA ** (-jnp.arange(0, ROT_DIM, 2, dtype=jnp.float32) / ROT_DIM)
    ang = pos[:, None] * inv[None, :]
    cos, sin = jnp.cos(ang), jnp.sin(ang)
    lane = jnp.arange(LANES) % HEAD_DIM
    is1 = lane < half
    is2 = (lane >= half) & (lane < ROT_DIM)
    fi = jnp.where(is1, lane, jnp.where(is2, lane - half, 0))
    cn = jnp.where(is1 | is2, cos[:, fi], 1.0)
    s1 = jnp.where(is1, -sin[:, fi], 0.0)
    s2 = jnp.where(is2, sin[:, fi], 0.0)
    return cos.T, sin.T, cn, s1, s2


def _qkv(h, wq, wk, wv):
    B, lp, D = h.shape
    nt = lp // TM
    cosT, sinT, cn, s1, s2 = _rope_tables(lp)
    full2 = lambda b, i: (0, 0)
    hd2 = 2 * HEAD_DIM
    out5 = lambda r, c: jax.ShapeDtypeStruct((B, N_HEADS, nt, r, c), jnp.bfloat16)
    spec5 = lambda r, c: pl.BlockSpec((1, N_HEADS, 1, r, c), lambda b, i: (b, 0, i, 0, 0))
    return pl.pallas_call(
        _qkv_kernel,
        grid=(B, nt),
        in_specs=[
            pl.BlockSpec((1, TM, D), lambda b, i: (b, i, 0)),
            pl.BlockSpec((D, D), full2),
            pl.BlockSpec((D, D), full2),
            pl.BlockSpec((D, D), full2),
            pl.BlockSpec((ROT_DIM // 2, TM), lambda b, i: (0, i)),
            pl.BlockSpec((ROT_DIM // 2, TM), lambda b, i: (0, i)),
            pl.BlockSpec((TM, LANES), lambda b, i: (i, 0)),
            pl.BlockSpec((TM, LANES), lambda b, i: (i, 0)),
            pl.BlockSpec((TM, LANES), lambda b, i: (i, 0)),
        ],
        out_specs=[spec5(hd2, TM), spec5(TM, hd2), spec5(V_ROWS, TM)],
        out_shape=[out5(hd2, TM), out5(TM, hd2), out5(V_ROWS, TM)],
        compiler_params=_cparams(("parallel", "parallel")),
        name="qkv_proj",
    )(h, wq.T.astype(jnp.bfloat16), wk.astype(jnp.bfloat16), wv.T.astype(jnp.bfloat16), cosT, sinT, cn, s1, s2)


MASK_PAD, MASK_DIAG, MASK_ALL = 1, 2, 4
ATT_UNROLL = 8
SOFTMAX_ROWS = 32


MASK_VALUE = -1e30
V_ROWS = V_DIM + 16
N_CHUNK_T = TM // CHUNK


def _attn_mask_tables():
    krow = jnp.arange(TM)[:, None]
    col = jnp.arange(LANES)[None, :]
    diag = (col < N_CHUNK_T) & (krow // CHUNK == col)
    pad = (col == N_CHUNK_T) & (krow < PAD0)
    everything = jnp.broadcast_to(col == N_CHUNK_T + 1, (TM, LANES))
    none = jnp.zeros((TM, LANES), bool)
    sel = jnp.stack([none, pad, diag, pad | diag, everything]).astype(jnp.bfloat16)
    row = jnp.arange(LANES)[:, None]
    qchunk = (jnp.arange(2 * TM)[None, :] % TM) // CHUNK
    hidden = ((row < N_CHUNK_T) & (row > qchunk)) | (row == N_CHUNK_T) | (row == N_CHUNK_T + 1)
    val = jnp.where(hidden, MASK_VALUE, 0.0).astype(jnp.bfloat16)
    return sel, val


def _attn_kernel(lam_init, q_ref, k_ref, v_ref, sel_ref, val_ref, lam_ref, g_ref, o_ref,
                 qcat_ref, s_ref, mt_ref, p_ref, alpha_ref, m_ref, acc_ref):
    qi = pl.program_id(2)
    W = 2 * TM
    F = 2 * HEAD_DIM
    q = q_ref[0, 0, 0]
    zero = jnp.zeros((HEAD_DIM, TM), jnp.bfloat16)
    qcat_ref[0:F, 0:TM] = jnp.concatenate([q[:HEAD_DIM], zero], axis=0)
    qcat_ref[0:F, TM:W] = jnp.concatenate([zero, q[HEAD_DIM:]], axis=0)
    qcat_ref[F:2 * F, :] = val_ref[...]
    s_ref[1] = jnp.full((TM, W), -jnp.inf, jnp.float32)
    p_ref[0] = jnp.zeros((TM, W), jnp.bfloat16)
    alpha_ref[0] = jnp.ones((8, W), jnp.float32)
    mt_ref[1] = jnp.full((8, W), -jnp.inf, jnp.float32)
    m_ref[...] = jnp.full(m_ref.shape, jnp.finfo(jnp.float32).min, jnp.float32)
    acc_ref[...] = jnp.zeros_like(acc_ref)

    def stages(t, cur):
        prv = 1 - cur
        kind = jnp.where(t > qi, MASK_ALL, jnp.where(t == 0, MASK_PAD, 0) + jnp.where(t == qi, MASK_DIAG, 0))
        lhs = jnp.concatenate([k_ref[0, 0, jnp.minimum(t, qi)], sel_ref[kind]], axis=1)
        s_new = jnp.dot(lhs, qcat_ref[...], preferred_element_type=jnp.float32)
        s_ref[cur] = s_new
        mt_ref[cur, 0:1, :] = jnp.max(s_new, axis=0, keepdims=True)
        vt = v_ref[0, 0, jnp.clip(t - 2, 0, qi)]
        pv = jnp.dot(vt, p_ref[cur], preferred_element_type=jnp.float32)
        acc_ref[...] = alpha_ref[cur, 0:1, :] * acc_ref[...] + pv
        m_old = m_ref[0:1, :]
        m_new = jnp.maximum(m_old, mt_ref[prv, 0:1, :])
        for r in range(0, TM, SOFTMAX_ROWS):
            p = jnp.exp2(s_ref[prv, r:r + SOFTMAX_ROWS, :] - m_new)
            p_ref[prv, r:r + SOFTMAX_ROWS, :] = p.astype(jnp.bfloat16)
        m_ref[0:1, :] = m_new
        alpha_ref[prv, 0:1, :] = jnp.exp2(m_old - m_new)

    def body(i, carry):
        for u in range(ATT_UNROLL):
            stages(i * ATT_UNROLL + u, u & 1)
        return carry

    lax.fori_loop(0, (qi + 2 + ATT_UNROLL) // ATT_UNROLL, body, 0)

    lp = lam_ref[...]
    lam = (jnp.exp(jnp.sum(lp[0:1] * lp[1:2], axis=1, keepdims=True))
           - jnp.exp(jnp.sum(lp[2:3] * lp[3:4], axis=1, keepdims=True)) + lam_init)
    l = acc_ref[V_DIM:V_DIM + 1, :]
    l = jnp.where(l == 0.0, 1.0, l)
    acc = acc_ref[0:V_DIM, :]
    o = acc[:, 0:TM] / l[:, 0:TM] - lam * (acc[:, TM:W] / l[:, TM:W])
    ms = jnp.mean(o * o, axis=0, keepdims=True)
    y = o * lax.rsqrt(ms + EPS) * g_ref[...] * (1.0 - lam_init)
    o_ref[0] = y.T.astype(o_ref.dtype)


def _attention(q5, k5, v5, lam_params, subln_g, lam_init):
    B, H, nt, _, _ = q5.shape
    lp = nt * TM
    kernel = functools.partial(_attn_kernel, lam_init)
    sel, val = _attn_mask_tables()
    return pl.pallas_call(
        kernel,
        grid=(B, H, nt),
        in_specs=[
            pl.BlockSpec((1, 1, 1, 2 * HEAD_DIM, TM), lambda b, h, i: (b, h, i, 0, 0)),
            pl.BlockSpec((1, 1, nt, TM, 2 * HEAD_DIM), lambda b, h, i: (b, h, 0, 0, 0)),
            pl.BlockSpec((1, 1, nt, V_ROWS, TM), lambda b, h, i: (b, h, 0, 0, 0)),
            pl.BlockSpec((5, TM, LANES), lambda b, h, i: (0, 0, 0)),
            pl.BlockSpec((LANES, 2 * TM), lambda b, h, i: (0, 0)),
            pl.BlockSpec((4, HEAD_DIM), lambda b, h, i: (0, 0)),
            pl.BlockSpec((V_DIM, 1), lambda b, h, i: (0, 0)),
        ],
        out_specs=pl.BlockSpec((1, TM, V_DIM), lambda b, h, i: (b, i, h)),
        out_shape=jax.ShapeDtypeStruct((B, lp, H * V_DIM), jnp.bfloat16),
        scratch_shapes=[
            pltpu.VMEM((4 * HEAD_DIM, 2 * TM), jnp.bfloat16),
            pltpu.VMEM((2, TM, 2 * TM), jnp.float32),
            pltpu.VMEM((2, 8, 2 * TM), jnp.float32),
            pltpu.VMEM((2, TM, 2 * TM), jnp.bfloat16),
            pltpu.VMEM((2, 8, 2 * TM), jnp.float32),
            pltpu.VMEM((8, 2 * TM), jnp.float32),
            pltpu.VMEM((V_ROWS, 2 * TM), jnp.float32),
        ],
        compiler_params=_cparams(("parallel", "parallel", "arbitrary")),
        name="diff_attention",
    )(q5, k5, v5, sel, val, lam_params, subln_g.reshape(V_DIM, 1))


def _attn_out_kernel(o_ref, w_ref, h_ref, g_ref, b_ref, out_ref):
    m = jnp.dot(o_ref[0], w_ref[...], preferred_element_type=jnp.float32)
    out_ref[0] = _layernorm(ALPHA * h_ref[0] + m, g_ref[...], b_ref[...])


def _attn_out(o, w_o, h, g, b):
    B, lp, D = h.shape
    nt = lp // TM
    full2 = lambda bb, i: (0, 0)
    tile = pl.BlockSpec((1, TM, D), lambda bb, i: (bb, i, 0))
    return pl.pallas_call(
        _attn_out_kernel,
        grid=(B, nt),
        in_specs=[tile, pl.BlockSpec((D, D), full2), tile, pl.BlockSpec((1, D), full2), pl.BlockSpec((1, D), full2)],
        out_specs=tile,
        out_shape=jax.ShapeDtypeStruct((B, lp, D), jnp.float32),
        compiler_params=_cparams(("parallel", "parallel")),
        name="attn_out_ln",
    )(o, w_o.astype(jnp.bfloat16), h, g.reshape(1, D), b.reshape(1, D))


def kernel(x, meta_tokens, conv_w_pw1, conv_b_pw1, conv_w_dw, conv_b_dw, conv_ln_g, conv_ln_b, conv_w_pw2, conv_b_pw2, kv_w_k, kv_w_v, attn_w_q, attn_lam_q1, attn_lam_k1, attn_lam_q2, attn_lam_k2, attn_subln_g, attn_w_o, post_ln_g, post_ln_b, moe_w_group, moe_b_group, moe_w_expert, moe_b_expert, moe_w1, moe_w3, moe_w2):
    B, S, D = x.shape
    assert D == D_MODEL and S % TM == 0 and S % CHUNK == 0
    assert conv_w_pw1.shape[0] == 1 and attn_w_q.shape[0] == 1 and post_ln_g.shape[0] == DEPTH
    lp = S + TM

    def moe(h, l, drop_front):
        out = _moe_layer(h.reshape(B * lp, D), moe_w_group[l], moe_b_group[l], moe_w_expert[l],
                         moe_b_expert[l], moe_w1[l], moe_w3[l], moe_w2[l],
                         post_ln_g[l, 1], post_ln_b[l, 1], B, drop_front)
        return out.reshape(B, -1, D)

    h = _conv_layer(x, meta_tokens, conv_w_pw1[0], conv_b_pw1[0], conv_w_dw[0], conv_b_dw[0],
                    conv_ln_g[0], conv_ln_b[0], conv_w_pw2[0], conv_b_pw2[0],
                    post_ln_g[0, 0], post_ln_b[0, 0])
    h = moe(h, 0, False)
    q5, k5, v5 = _qkv(h, attn_w_q[0], kv_w_k, kv_w_v)
    lam_init = 0.8 - 0.6 * math.exp(-0.3 * 1)
    lam_params = jnp.stack([attn_lam_q1[0], attn_lam_k1[0], attn_lam_q2[0], attn_lam_k2[0]])
    o = _attention(q5, k5, v5, lam_params, attn_subln_g[0], lam_init)
    h = _attn_out(o, attn_w_o[0], h, post_ln_g[1, 0], post_ln_b[1, 0])
    return moe(h, 1, True)
```

```python
import functools
import math

import jax
import jax.numpy as jnp
from jax import lax
from jax.experimental import pallas as pl
from jax.experimental.pallas import tpu as pltpu

D_MODEL = 1024
N_META = 16
CHUNK = 64
DEPTH = 2
ALPHA = (2.0 * DEPTH) ** 0.25
CONV_W = 31
N_HEADS = 8
HEAD_DIM = 64
V_DIM = 128
ROT_DIM = 16
ROPE_THETA = 500000.0
N_GROUPS = 4
EXPERTS_PER_GROUP = 4
N_EXPERTS = 16
D_EXPERT = 512
EPS = 1e-5

TM = 256
PAD0 = TM - N_META
HALO = 32
CONV_SLACK = 8
TG = 256
ROW_TILE = 8
MOVE_ROWS = 512
LANES = 128
NEG = -1e30
VMEM_LIMIT = 48 * 1024 * 1024


def _cparams(sem):
    return pltpu.CompilerParams(dimension_semantics=sem, vmem_limit_bytes=VMEM_LIMIT)


def _layernorm(x, g, b):
    mu = jnp.mean(x, axis=-1, keepdims=True)
    xc = x - mu
    var = jnp.mean(xc * xc, axis=-1, keepdims=True)
    return xc * lax.rsqrt(var + EPS) * g + b


def _sigmoid(x):
    return 1.0 / (1.0 + jnp.exp(-x))


def _front_tile(meta_ref):
    return jnp.concatenate([jnp.zeros((PAD0, D_MODEL), jnp.float32), meta_ref[...]], axis=0)


def _stream_tile(i, x_ref, meta_ref):
    return jnp.where(i == 0, _front_tile(meta_ref), x_ref[0])


def _conv_in_kernel(x_ref, meta_ref, w_ref, b_ref, u_ref):
    i = pl.program_id(1)
    xt = _stream_tile(i, x_ref, meta_ref)
    h = jnp.dot(xt.astype(jnp.bfloat16), w_ref[...], preferred_element_type=jnp.float32) + b_ref[...]
    u = h[:, :D_MODEL] * _sigmoid(h[:, D_MODEL:])
    row = lax.broadcasted_iota(jnp.int32, (TM, 1), 0)
    u_ref[0] = jnp.where((i == 0) & (row < PAD0), 0.0, u)


def _conv_mid_kernel(ucur_ref, uprev_ref, x_ref, meta_ref, wdw_ref, bdw_ref, lng_ref, lnb_ref,
                     w2_ref, b2_ref, pg_ref, pb_ref, h_ref, win_ref, part_ref, conv_ref):
    i = pl.program_id(1)
    win_ref[0:HALO, :] = jnp.where(i == 0, 0.0, uprev_ref[0])
    win_ref[HALO:HALO + TM, :] = ucur_ref[0]
    win_ref[HALO + TM:, :] = jnp.zeros((CONV_SLACK, D_MODEL), jnp.float32)
    off = HALO - (CONV_W - 1)
    rows = TM + 16
    for c in range(D_MODEL // LANES):
        cs = slice(c * LANES, (c + 1) * LANES)
        acc = jnp.zeros((TM, LANES), jnp.float32)
        for r in range(8):
            part = None
            for j in range(r, CONV_W, 8):
                term = win_ref[j - r:j - r + rows, cs] * wdw_ref[j:j + 1, cs]
                part = term if part is None else part + term
            slot = (c * 8 + r) % 2
            part_ref[slot] = part
            acc = acc + part_ref[slot, off + r:off + r + TM, :]
        conv_ref[:, cs] = acc
    y = conv_ref[...] + bdw_ref[...]
    y = _layernorm(y, lng_ref[...], lnb_ref[...])
    y = y * _sigmoid(y)
    m = jnp.dot(y.astype(jnp.bfloat16), w2_ref[...], preferred_element_type=jnp.float32) + b2_ref[...]
    h0 = _stream_tile(i, x_ref, meta_ref)
    h_ref[0] = _layernorm(ALPHA * h0 + m, pg_ref[...], pb_ref[...])


def _conv_layer(x, meta, w1, b1, wdw, bdw, lng, lnb, w2, b2, pg, pb):
    B, S, D = x.shape
    nt = S // TM + 1
    lp = nt * TM
    xmap = lambda b, i: (b, jnp.maximum(i - 1, 0), 0)
    full2 = lambda b, i: (0, 0)
    u = pl.pallas_call(
        _conv_in_kernel,
        grid=(B, nt),
        in_specs=[
            pl.BlockSpec((1, TM, D), xmap),
            pl.BlockSpec((N_META, D), full2),
            pl.BlockSpec((D, 2 * D), full2),
            pl.BlockSpec((1, 2 * D), full2),
        ],
        out_specs=pl.BlockSpec((1, TM, D), lambda b, i: (b, i, 0)),
        out_shape=jax.ShapeDtypeStruct((B, lp, D), jnp.float32),
        compiler_params=_cparams(("parallel", "parallel")),
        name="conv_in",
    )(x, meta, w1.astype(jnp.bfloat16), b1.reshape(1, -1))
    wdw_p = jnp.concatenate([wdw, jnp.zeros((HALO - CONV_W, D), wdw.dtype)], axis=0)
    vec = lambda a: a.reshape(1, D)
    h1 = pl.pallas_call(
        _conv_mid_kernel,
        grid=(B, nt),
        in_specs=[
            pl.BlockSpec((1, TM, D), lambda b, i: (b, i, 0)),
            pl.BlockSpec((1, HALO, D), lambda b, i: (b, jnp.maximum(i * (TM // HALO) - 1, 0), 0)),
            pl.BlockSpec((1, TM, D), xmap),
            pl.BlockSpec((N_META, D), full2),
            pl.BlockSpec((HALO, D), full2),
            pl.BlockSpec((1, D), full2),
            pl.BlockSpec((1, D), full2),
            pl.BlockSpec((1, D), full2),
            pl.BlockSpec((D, D), full2),
            pl.BlockSpec((1, D), full2),
            pl.BlockSpec((1, D), full2),
            pl.BlockSpec((1, D), full2),
        ],
        out_specs=pl.BlockSpec((1, TM, D), lambda b, i: (b, i, 0)),
        out_shape=jax.ShapeDtypeStruct((B, lp, D), jnp.float32),
        scratch_shapes=[
            pltpu.VMEM((HALO + TM + CONV_SLACK, D), jnp.float32),
            pltpu.VMEM((2, TM + 16, LANES), jnp.float32),
            pltpu.VMEM((TM, D), jnp.float32),
        ],
        compiler_params=_cparams(("parallel", "parallel")),
        name="conv_mid",
    )(u, u, x, meta, wdw_p, vec(bdw), vec(lng), vec(lnb), w2.astype(jnp.bfloat16), vec(b2), vec(pg), vec(pb))
    return h1


def _route_kernel(h_ref, whi_ref, wlo_ref, b_ref, info_ref, cnt_ref, htile_ref, carry_ref):
    step = pl.program_id(0)

    @pl.when(step == 0)
    def _():
        carry_ref[...] = jnp.zeros_like(carry_ref)

    h = h_ref[...]
    _to_token_tiles(htile_ref, h)
    hhi = h.astype(jnp.bfloat16)
    hlo = (h - hhi.astype(jnp.float32)).astype(jnp.bfloat16)
    lg = (jnp.dot(hhi, whi_ref[...], preferred_element_type=jnp.float32)
          + jnp.dot(hlo, whi_ref[...], preferred_element_type=jnp.float32)
          + jnp.dot(hhi, wlo_ref[...], preferred_element_type=jnp.float32)) + b_ref[...]
    lane_i = lax.broadcasted_iota(jnp.int32, (TM, LANES), 1)
    lane = lane_i.astype(jnp.float32)

    def first_argmax(v, vmax):
        return jnp.min(jnp.where(v == vmax, lane, float(LANES)), axis=1, keepdims=True)

    gmask = lane_i < N_GROUPS
    glog = jnp.where(gmask, lg, NEG)
    gmax = jnp.max(glog, axis=1, keepdims=True)
    gidx = first_argmax(glog, gmax)
    gsum = jnp.sum(jnp.where(gmask, jnp.exp(glog - gmax), 0.0), axis=1, keepdims=True)
    g_w = 1.0 / gsum
    elane = lane_i - N_GROUPS
    egrp = (elane >> 2).astype(jnp.float32)
    emask = (elane >= 0) & (elane < N_EXPERTS) & (egrp == gidx)
    elog = jnp.where(emask, lg, NEG)
    e1 = jnp.max(elog, axis=1, keepdims=True)
    i1 = first_argmax(elog, e1)
    elog2 = jnp.where(lane == i1, NEG, elog)
    e2 = jnp.max(elog2, axis=1, keepdims=True)
    i2 = first_argmax(elog2, e2)
    d = jnp.exp(e2 - e1)
    w1 = g_w / (1.0 + d)
    w2 = g_w * d / (1.0 + d)
    x1 = i1 - float(N_GROUPS)
    x2 = i2 - float(N_GROUPS)
    oh1 = jnp.where(lane == x1, 1.0, 0.0)
    oh2 = jnp.where(lane == x2, 1.0, 0.0)
    r_i = lax.broadcasted_iota(jnp.int32, (TM, TM), 0)
    c_i = lax.broadcasted_iota(jnp.int32, (TM, TM), 1)
    ltri = jnp.where(c_i < r_i, 1.0, 0.0).astype(jnp.bfloat16)
    pre1 = jnp.dot(ltri, oh1.astype(jnp.bfloat16), preferred_element_type=jnp.float32)
    pre2 = jnp.dot(ltri, oh2.astype(jnp.bfloat16), preferred_element_type=jnp.float32)
    cnt1 = jnp.sum(oh1, axis=0, keepdims=True)
    cnt2 = jnp.sum(oh2, axis=0, keepdims=True)
    carry = carry_ref[0:1, :]
    rank1 = jnp.sum(oh1 * (carry + pre1), axis=1, keepdims=True)
    rank2 = jnp.sum(oh2 * (carry + cnt1 + pre2), axis=1, keepdims=True)
    new_carry = carry + cnt1 + cnt2
    carry_ref[...] = jnp.broadcast_to(new_carry, carry_ref.shape)
    cnt_ref[...] = jnp.broadcast_to(new_carry, cnt_ref.shape)
    info = jnp.where(lane_i == 0, x1, 0.0)
    info = jnp.where(lane_i == 1, x2, info)
    info = jnp.where(lane_i == 2, w1, info)
    info = jnp.where(lane_i == 3, w2, info)
    info = jnp.where(lane_i == 4, rank1, info)
    info = jnp.where(lane_i == 5, rank2, info)
    info_ref[...] = info


def _route(h2d, w_group, b_group, w_expert, b_expert):
    T, D = h2d.shape
    wr = jnp.zeros((D, LANES), jnp.float32)
    wr = wr.at[:, :N_GROUPS].set(w_group).at[:, N_GROUPS:N_GROUPS + N_EXPERTS].set(w_expert)
    br = jnp.zeros((1, LANES), jnp.float32)
    br = br.at[0, :N_GROUPS].set(b_group).at[0, N_GROUPS:N_GROUPS + N_EXPERTS].set(b_expert)
    whi = wr.astype(jnp.bfloat16)
    wlo = (wr - whi.astype(jnp.float32)).astype(jnp.bfloat16)
    return pl.pallas_call(
        _route_kernel,
        grid=(T // TM,),
        in_specs=[
            pl.BlockSpec((TM, D), lambda i: (i, 0)),
            pl.BlockSpec((D, LANES), lambda i: (0, 0)),
            pl.BlockSpec((D, LANES), lambda i: (0, 0)),
            pl.BlockSpec((1, LANES), lambda i: (0, 0)),
        ],
        out_specs=[
            pl.BlockSpec((TM, LANES), lambda i: (i, 0)),
            pl.BlockSpec((8, LANES), lambda i: (0, 0)),
            pl.BlockSpec((TM * ROW_TILE, LANES), lambda i: (i, 0)),
        ],
        out_shape=[
            jax.ShapeDtypeStruct((T, LANES), jnp.float32),
            jax.ShapeDtypeStruct((8, LANES), jnp.float32),
            jax.ShapeDtypeStruct((T * ROW_TILE, LANES), jnp.float32),
        ],
        scratch_shapes=[pltpu.VMEM((8, LANES), jnp.float32)],
        compiler_params=_cparams(("arbitrary",)),
        name="route",
    )(h2d, whi, wlo, br)


def _dispatch_plan(info, cnt, n_rows):
    ex = info[:, 0:2].astype(jnp.int32)
    rank = info[:, 4:6].astype(jnp.int32)
    counts = cnt[0, :N_EXPERTS].astype(jnp.int32)
    padded = ((counts + TG - 1) // TG) * TG
    ends = jnp.cumsum(padded)
    starts = ends - padded
    pos = starts[ex] + rank
    i = jnp.arange(TG, dtype=jnp.int32)[None, :]
    pad_rows = jnp.where(i < (padded - counts)[:, None], (starts + counts)[:, None] + i, -1)
    tile_start = jnp.arange(n_rows // TG, dtype=jnp.int32) * TG
    tile_expert = jnp.minimum(
        jnp.sum((tile_start[:, None] >= ends[None, :]).astype(jnp.int32), axis=1), N_EXPERTS - 1)
    n_used = (ends[-1] // TG).astype(jnp.int32).reshape(1)
    return pos, pad_rows, tile_expert.astype(jnp.int32), n_used


def _to_token_tiles(ref, x):
    rows = x.shape[0]
    for j in range(ROW_TILE):
        ref[pl.ds(j, rows, stride=ROW_TILE), :] = x[:, j * LANES:(j + 1) * LANES]


def _from_token_tiles(ref, rows):
    return jnp.concatenate([ref[pl.ds(j, rows, stride=ROW_TILE), :] for j in range(ROW_TILE)], axis=1)


def _issue_row_copies(copy_pair, n_pairs):
    def start(i, c):
        a, b = copy_pair(i)
        a.start(priority=0)
        b.start(priority=1)
        return c

    def wait(i, c):
        a, b = copy_pair(i)
        a.wait()
        b.wait()
        return c

    lax.fori_loop(0, n_pairs, start, 0, unroll=4)
    lax.fori_loop(0, n_pairs, wait, 0, unroll=4)


def _gather_kernel(src_ref, table_ref, out_ref, sem):
    def row_copy(r):
        return pltpu.make_async_copy(table_ref.at[src_ref[0, 0, r]], out_ref.at[r], sem)

    _issue_row_copies(lambda i: (row_copy(2 * i), row_copy(2 * i + 1)), MOVE_ROWS // 2)


def _gather_rows(table, src):
    n = src.shape[0]
    return pl.pallas_call(
        _gather_kernel,
        grid=(n // MOVE_ROWS,),
        in_specs=[
            pl.BlockSpec((1, 1, MOVE_ROWS), lambda i: (i, 0, 0), memory_space=pltpu.SMEM),
            pl.BlockSpec(memory_space=pl.ANY),
        ],
        out_specs=pl.BlockSpec((MOVE_ROWS,) + table.shape[1:], lambda i: (i, 0, 0)),
        out_shape=jax.ShapeDtypeStruct((n,) + table.shape[1:], table.dtype),
        scratch_shapes=[pltpu.SemaphoreType.DMA(())],
        compiler_params=_cparams(("arbitrary",)),
        name="gather_rows",
    )(src.reshape(n // MOVE_ROWS, 1, MOVE_ROWS), table)


def _dispatch_kernel(n_tiles, pads_per_step, tails_per_step, dst_ref, pad_ref, nu_ref, table_ref,
                     out_ref, zrow_ref, ztile_ref, sem, zsem):
    step = pl.program_id(0)

    @pl.when(step == 0)
    def _():
        zrow_ref[...] = jnp.zeros_like(zrow_ref)
        ztile_ref[...] = jnp.zeros_like(ztile_ref)

    def row_copy(i, k):
        return pltpu.make_async_copy(table_ref.at[i], out_ref.at[dst_ref[0, 0, 2 * i + k]], sem)

    _issue_row_copies(lambda i: (row_copy(i, 0), row_copy(i, 1)), MOVE_ROWS // 2)

    def tail_copy(k):
        tail = nu_ref[0] + step * tails_per_step + k
        copy = pltpu.make_async_copy(ztile_ref, out_ref.at[pl.ds(jnp.minimum(tail, n_tiles - 1) * TG, TG)], zsem)
        return tail < n_tiles, copy

    def pad_copy(r):
        return pltpu.make_async_copy(zrow_ref, out_ref.at[jnp.maximum(pad_ref[0, 0, r], 0)], zsem)

    def start(r, c):
        @pl.when(pad_ref[0, 0, r] >= 0)
        def _():
            pad_copy(r).start()
        return c

    def wait(r, c):
        @pl.when(pad_ref[0, 0, r] >= 0)
        def _():
            pad_copy(r).wait()
        return c

    for k in range(tails_per_step):
        live, copy = tail_copy(k)
        pl.when(live)(copy.start)
    lax.fori_loop(0, pads_per_step, start, 0)
    lax.fori_loop(0, pads_per_step, wait, 0)
    for k in range(tails_per_step):
        live, copy = tail_copy(k)
        pl.when(live)(copy.wait)


def _dispatch_rows(table, dst, pad_rows, n_used, n_out):
    n = dst.size
    steps = n // MOVE_ROWS
    pads_per_step = pl.cdiv(pad_rows.size, steps)
    tails_per_step = pl.cdiv(n_out // TG - n // TG, steps)
    pads = jnp.concatenate([pad_rows.reshape(-1),
                            jnp.full((steps * pads_per_step - pad_rows.size,), -1, jnp.int32)])
    return pl.pallas_call(
        functools.partial(_dispatch_kernel, n_out // TG, pads_per_step, tails_per_step),
        grid=(steps,),
        in_specs=[
            pl.BlockSpec((1, 1, MOVE_ROWS), lambda i: (i, 0, 0), memory_space=pltpu.SMEM),
            pl.BlockSpec((1, 1, pads_per_step), lambda i: (i, 0, 0), memory_space=pltpu.SMEM),
            pl.BlockSpec(memory_space=pltpu.SMEM),
            pl.BlockSpec((MOVE_ROWS // 2,) + table.shape[1:], lambda i: (i, 0, 0)),
        ],
        out_specs=pl.BlockSpec(memory_space=pl.ANY),
        out_shape=jax.ShapeDtypeStruct((n_out,) + table.shape[1:], table.dtype),
        scratch_shapes=[
            pltpu.VMEM(table.shape[1:], table.dtype),
            pltpu.VMEM((TG,) + table.shape[1:], table.dtype),
            pltpu.SemaphoreType.DMA(()),
            pltpu.SemaphoreType.DMA(()),
        ],
        compiler_params=_cparams(("arbitrary",)),
        name="dispatch_rows",
    )(dst.reshape(steps, 1, MOVE_ROWS), pads.reshape(steps, 1, pads_per_step), n_used, table)


def _gmm_kernel(te_ref, nu_ref, x_ref, w1_ref, w3_ref, w2_ref, y_ref):
    j = pl.program_id(0)

    @pl.when(j < nu_ref[0])
    def _():
        x = _from_token_tiles(x_ref, TG).astype(jnp.bfloat16)
        a = jnp.dot(x, w1_ref[0], preferred_element_type=jnp.float32)
        g = jnp.dot(x, w3_ref[0], preferred_element_type=jnp.float32)
        hh = (a * _sigmoid(a) * g).astype(jnp.bfloat16)
        _to_token_tiles(y_ref, jnp.dot(hh, w2_ref[0], preferred_element_type=jnp.float32))

    @pl.when(j >= nu_ref[0])
    def _():
        y_ref[...] = jnp.zeros_like(y_ref)


def _gmm(xs2d, n_rows, w1, w3, w2, tile_expert, n_used):
    D, de = w1.shape[1], w1.shape[2]
    blk = pl.BlockSpec((TG * ROW_TILE, LANES), lambda j, te, nu: (j, 0))
    grid_spec = pltpu.PrefetchScalarGridSpec(
        num_scalar_prefetch=2,
        grid=(n_rows // TG,),
        in_specs=[
            blk,
            pl.BlockSpec((1, D, de), lambda j, te, nu: (te[j], 0, 0)),
            pl.BlockSpec((1, D, de), lambda j, te, nu: (te[j], 0, 0)),
            pl.BlockSpec((1, de, D), lambda j, te, nu: (te[j], 0, 0)),
        ],
        out_specs=blk,
    )
    return pl.pallas_call(
        _gmm_kernel,
        grid_spec=grid_spec,
        out_shape=jax.ShapeDtypeStruct((n_rows * ROW_TILE, LANES), jnp.float32),
        compiler_params=_cparams(("arbitrary",)),
        name="expert_mlp",
    )(tile_expert, n_used, xs2d, w1, w3, w2)


def _combine_kernel(h_ref, y1_ref, y2_ref, info_ref, g_ref, b_ref, o_ref):
    info = info_ref[...]
    f = info[:, 2:3] * _from_token_tiles(y1_ref, TM) + info[:, 3:4] * _from_token_tiles(y2_ref, TM)
    o_ref[...] = _layernorm(ALPHA * h_ref[...] + f, g_ref[...], b_ref[...])


def _combine_ln(h2d, yg2d, info, g, b, batch, drop_front):
    T, D = h2d.shape
    nt = T // TM
    ntb = nt // batch
    if drop_front:
        tile = lambda bb, i: bb * ntb + i + 1
        grid = (batch, ntb - 1)
        out_rows = batch * (ntb - 1) * TM
        omap = lambda bb, i: (bb * (ntb - 1) + i, 0)
    else:
        tile = lambda bb, i: bb * ntb + i
        grid = (batch, ntb)
        out_rows = T
        omap = lambda bb, i: (bb * ntb + i, 0)
    return pl.pallas_call(
        _combine_kernel,
        grid=grid,
        in_specs=[
            pl.BlockSpec((TM, D), lambda bb, i: (tile(bb, i), 0)),
            pl.BlockSpec((TM * ROW_TILE, LANES), lambda bb, i: (tile(bb, i), 0)),
            pl.BlockSpec((TM * ROW_TILE, LANES), lambda bb, i: (nt + tile(bb, i), 0)),
            pl.BlockSpec((TM, LANES), lambda bb, i: (tile(bb, i), 0)),
            pl.BlockSpec((1, D), lambda bb, i: (0, 0)),
            pl.BlockSpec((1, D), lambda bb, i: (0, 0)),
        ],
        out_specs=pl.BlockSpec((TM, D), omap),
        out_shape=jax.ShapeDtypeStruct((out_rows, D), jnp.float32),
        compiler_params=_cparams(("parallel", "parallel")),
        name="moe_combine_ln",
    )(h2d, yg2d, yg2d, info, g.reshape(1, D), b.reshape(1, D))


def _moe_layer(h2d, wg, bg, we, be, w1, w3, w2, pg, pb, batch, drop_front):
    T, D = h2d.shape
    n_rows = 2 * T + N_EXPERTS * TG
    info, cnt, h_tiles = _route(h2d, wg, bg, we, be)
    pos, pad_rows, tile_expert, n_used = _dispatch_plan(info, cnt, n_rows)
    xs = _dispatch_rows(h_tiles.reshape(T, ROW_TILE, LANES), pos, pad_rows, n_used, n_rows)
    ys = _gmm(xs.reshape(-1, LANES), n_rows, w1.astype(jnp.bfloat16), w3.astype(jnp.bfloat16),
              w2.astype(jnp.bfloat16), tile_expert, n_used)
    yg = _gather_rows(ys.reshape(n_rows, ROW_TILE, LANES), jnp.concatenate([pos[:, 0], pos[:, 1]]))
    return _combine_ln(h2d, yg.reshape(-1, LANES), info, pg, pb, batch, drop_front)


def _qkv_kernel(h_ref, wq_ref, wk_ref, wv_ref, cosT_ref, sinT_ref, cn_ref, s1_ref, s2_ref,
                q_ref, k_ref, v_ref):
    hb = h_ref[0].astype(jnp.bfloat16)
    half = ROT_DIM // 2
    nt_dims = (((1,), (1,)), ((), ()))
    qT = lax.dot_general(wq_ref[...], hb, nt_dims, preferred_element_type=jnp.float32)
    cosT = cosT_ref[...]
    sinT = sinT_ref[...]
    scale = HEAD_DIM ** -0.5 * math.log2(math.e)
    for hh in range(N_HEADS):
        parts = []
        for c in range(2):
            r0 = hh * 2 * HEAD_DIM + c * HEAD_DIM
            t1 = qT[r0:r0 + half]
            t2 = qT[r0 + half:r0 + 2 * half]
            parts += [t1 * cosT - t2 * sinT, t1 * sinT + t2 * cosT, qT[r0 + 2 * half:r0 + HEAD_DIM]]
        q_ref[0, hh, 0] = (jnp.concatenate(parts, axis=0) * scale).astype(jnp.bfloat16)
    k = jnp.dot(hb, wk_ref[...], preferred_element_type=jnp.float32)
    cn = cn_ref[...]
    s1 = s1_ref[...]
    s2 = s2_ref[...]
    for hh in range(N_HEADS):
        kb = k[:, hh * LANES:(hh + 1) * LANES]
        kr = kb * cn + pltpu.roll(kb, LANES - half, 1) * s1 + pltpu.roll(kb, half, 1) * s2
        k_ref[0, hh, 0] = kr.astype(jnp.bfloat16)
    vT = lax.dot_general(wv_ref[...], hb, nt_dims, preferred_element_type=jnp.float32)
    extra = lax.broadcasted_iota(jnp.int32, (V_ROWS - V_DIM, TM), 0)
    ones_row = jnp.where(extra == 0, 1.0, 0.0).astype(jnp.bfloat16)
    for hh in range(N_HEADS):
        v_ref[0, hh, 0, 0:V_DIM, :] = vT[hh * V_DIM:(hh + 1) * V_DIM].astype(jnp.bfloat16)
        v_ref[0, hh, 0, V_DIM:V_ROWS, :] = ones_row


def _rope_tables(lp):
    half = ROT_DIM // 2
    pos = jnp.arange(lp, dtype=jnp.float32) - PAD0
    inv = ROPE_THETA ** (-jnp.arange(0, ROT_DIM, 2, dtype=jnp.float32) / ROT_DIM)
    ang = pos[:, None] * inv[None, :]
    cos, sin = jnp.cos(ang), jnp.sin(ang)
    lane = jnp.arange(LANES) % HEAD_DIM
    is1 = lane < half
    is2 = (lane >= half) & (lane < ROT_DIM)
    fi = jnp.where(is1, lane, jnp.where(is2, lane - half, 0))
    cn = jnp.where(is1 | is2, cos[:, fi], 1.0)
    s1 = jnp.where(is1, -sin[:, fi], 0.0)
    s2 = jnp.where(is2, sin[:, fi], 0.0)
    return cos.T, sin.T, cn, s1, s2


def _qkv(h, wq, wk, wv):
    B, lp, D = h.shape
    nt = lp // TM
    cosT, sinT, cn, s1, s2 = _rope_tables(lp)
    full2 = lambda b, i: (0, 0)
    hd2 = 2 * HEAD_DIM
    out5 = lambda r, c: jax.ShapeDtypeStruct((B, N_HEADS, nt, r, c), jnp.bfloat16)
    spec5 = lambda r, c: pl.BlockSpec((1, N_HEADS, 1, r, c), lambda b, i: (b, 0, i, 0, 0))
    return pl.pallas_call(
        _qkv_kernel,
        grid=(B, nt),
        in_specs=[
            pl.BlockSpec((1, TM, D), lambda b, i: (b, i, 0)),
            pl.BlockSpec((D, D), full2),
            pl.BlockSpec((D, D), full2),
            pl.BlockSpec((D, D), full2),
            pl.BlockSpec((ROT_DIM // 2, TM), lambda b, i: (0, i)),
            pl.BlockSpec((ROT_DIM // 2, TM), lambda b, i: (0, i)),
            pl.BlockSpec((TM, LANES), lambda b, i: (i, 0)),
            pl.BlockSpec((TM, LANES), lambda b, i: (i, 0)),
            pl.BlockSpec((TM, LANES), lambda b, i: (i, 0)),
        ],
        out_specs=[spec5(hd2, TM), spec5(TM, hd2), spec5(V_ROWS, TM)],
        out_shape=[out5(hd2, TM), out5(TM, hd2), out5(V_ROWS, TM)],
        compiler_params=_cparams(("parallel", "parallel")),
        name="qkv_proj",
    )(h, wq.T.astype(jnp.bfloat16), wk.astype(jnp.bfloat16), wv.T.astype(jnp.bfloat16), cosT, sinT, cn, s1, s2)


MASK_PAD, MASK_DIAG, MASK_ALL = 1, 2, 4
ATT_UNROLL = 8
SOFTMAX_ROWS = 32


MASK_VALUE = -1e30
V_ROWS = V_DIM + 16
N_CHUNK_T = TM // CHUNK


def _attn_mask_tables():
    krow = jnp.arange(TM)[:, None]
    col = jnp.arange(LANES)[None, :]
    diag = (col < N_CHUNK_T) & (krow // CHUNK == col)
    pad = (col == N_CHUNK_T) & (krow < PAD0)
    everything = jnp.broadcast_to(col == N_CHUNK_T + 1, (TM, LANES))
    none = jnp.zeros((TM, LANES), bool)
    sel = jnp.stack([none, pad, diag, pad | diag, everything]).astype(jnp.bfloat16)
    row = jnp.arange(LANES)[:, None]
    qchunk = (jnp.arange(2 * TM)[None, :] % TM) // CHUNK
    hidden = ((row < N_CHUNK_T) & (row > qchunk)) | (row == N_CHUNK_T) | (row == N_CHUNK_T + 1)
    val = jnp.where(hidden, MASK_VALUE, 0.0).astype(jnp.bfloat16)
    return sel, val


def _attn_kernel(lam_init, q_ref, k_ref, v_ref, sel_ref, val_ref, lam_ref, g_ref, o_ref,
                 qcat_ref, s_ref, mt_ref, p_ref, alpha_ref, m_ref, acc_ref):
    qi = pl.program_id(2)
    W = 2 * TM
    F = 2 * HEAD_DIM
    q = q_ref[0, 0, 0]
    zero = jnp.zeros((HEAD_DIM, TM), jnp.bfloat16)
    qcat_ref[0:F, 0:TM] = jnp.concatenate([q[:HEAD_DIM], zero], axis=0)
    qcat_ref[0:F, TM:W] = jnp.concatenate([zero, q[HEAD_DIM:]], axis=0)
    qcat_ref[F:2 * F, :] = val_ref[...]
    s_ref[1] = jnp.full((TM, W), -jnp.inf, jnp.float32)
    p_ref[0] = jnp.zeros((TM, W), jnp.bfloat16)
    alpha_ref[0] = jnp.ones((8, W), jnp.float32)
    mt_ref[1] = jnp.full((8, W), -jnp.inf, jnp.float32)
    m_ref[...] = jnp.full(m_ref.shape, jnp.finfo(jnp.float32).min, jnp.float32)
    acc_ref[...] = jnp.zeros_like(acc_ref)

    def stages(t, cur):
        prv = 1 - cur
        kind = jnp.where(t > qi, MASK_ALL, jnp.where(t == 0, MASK_PAD, 0) + jnp.where(t == qi, MASK_DIAG, 0))
        lhs = jnp.concatenate([k_ref[0, 0, jnp.minimum(t, qi)], sel_ref[kind]], axis=1)
        s_new = jnp.dot(lhs, qcat_ref[...], preferred_element_type=jnp.float32)
        s_ref[cur] = s_new
        mt_ref[cur, 0:1, :] = jnp.max(s_new, axis=0, keepdims=True)
        vt = v_ref[0, 0, jnp.clip(t - 2, 0, qi)]
        pv = jnp.dot(vt, p_ref[cur], preferred_element_type=jnp.float32)
        acc_ref[...] = alpha_ref[cur, 0:1, :] * acc_ref[...] + pv
        m_old = m_ref[0:1, :]
        m_new = jnp.maximum(m_old, mt_ref[prv, 0:1, :])
        for r in range(0, TM, SOFTMAX_ROWS):
            p = jnp.exp2(s_ref[prv, r:r + SOFTMAX_ROWS, :] - m_new)
            p_ref[prv, r:r + SOFTMAX_ROWS, :] = p.astype(jnp.bfloat16)
        m_ref[0:1, :] = m_new
        alpha_ref[prv, 0:1, :] = jnp.exp2(m_old - m_new)

    def unrolled(first, unroll):
        def body(i, carry):
            for u in range(unroll):
                stages(first + i * unroll + u, u & 1)
            return carry
        return body

    n_iter = qi + 3
    n_main = n_iter // ATT_UNROLL
    lax.fori_loop(0, n_main, unrolled(0, ATT_UNROLL), 0)
    lax.fori_loop(0, (n_iter - n_main * ATT_UNROLL + 1) // 2, unrolled(n_main * ATT_UNROLL, 2), 0)

    lp = lam_ref[...]
    lam = (jnp.exp(jnp.sum(lp[0:1] * lp[1:2], axis=1, keepdims=True))
           - jnp.exp(jnp.sum(lp[2:3] * lp[3:4], axis=1, keepdims=True)) + lam_init)
    l = acc_ref[V_DIM:V_DIM + 1, :]
    l = jnp.where(l == 0.0, 1.0, l)
    acc = acc_ref[0:V_DIM, :]
    o = acc[:, 0:TM] / l[:, 0:TM] - lam * (acc[:, TM:W] / l[:, TM:W])
    ms = jnp.mean(o * o, axis=0, keepdims=True)
    y = o * lax.rsqrt(ms + EPS) * g_ref[...] * (1.0 - lam_init)
    o_ref[0] = y.T.astype(o_ref.dtype)


def _attention(q5, k5, v5, lam_params, subln_g, lam_init):
    B, H, nt, _, _ = q5.shape
    lp = nt * TM
    kernel = functools.partial(_attn_kernel, lam_init)
    sel, val = _attn_mask_tables()
    return pl.pallas_call(
        kernel,
        grid=(B, H, nt),
        in_specs=[
            pl.BlockSpec((1, 1, 1, 2 * HEAD_DIM, TM), lambda b, h, i: (b, h, i, 0, 0)),
            pl.BlockSpec((1, 1, nt, TM, 2 * HEAD_DIM), lambda b, h, i: (b, h, 0, 0, 0)),
            pl.BlockSpec((1, 1, nt, V_ROWS, TM), lambda b, h, i: (b, h, 0, 0, 0)),
            pl.BlockSpec((5, TM, LANES), lambda b, h, i: (0, 0, 0)),
            pl.BlockSpec((LANES, 2 * TM), lambda b, h, i: (0, 0)),
            pl.BlockSpec((4, HEAD_DIM), lambda b, h, i: (0, 0)),
            pl.BlockSpec((V_DIM, 1), lambda b, h, i: (0, 0)),
        ],
        out_specs=pl.BlockSpec((1, TM, V_DIM), lambda b, h, i: (b, i, h)),
        out_shape=jax.ShapeDtypeStruct((B, lp, H * V_DIM), jnp.bfloat16),
        scratch_shapes=[
            pltpu.VMEM((4 * HEAD_DIM, 2 * TM), jnp.bfloat16),
            pltpu.VMEM((2, TM, 2 * TM), jnp.float32),
            pltpu.VMEM((2, 8, 2 * TM), jnp.float32),
            pltpu.VMEM((2, TM, 2 * TM), jnp.bfloat16),
            pltpu.VMEM((2, 8, 2 * TM), jnp.float32),
            pltpu.VMEM((8, 2 * TM), jnp.float32),
            pltpu.VMEM((V_ROWS, 2 * TM), jnp.float32),
        ],
        compiler_params=_cparams(("parallel", "parallel", "arbitrary")),
        name="diff_attention",
    )(q5, k5, v5, sel, val, lam_params, subln_g.reshape(V_DIM, 1))


def _attn_out_kernel(o_ref, w_ref, h_ref, g_ref, b_ref, out_ref):
    m = jnp.dot(o_ref[0], w_ref[...], preferred_element_type=jnp.float32)
    out_ref[0] = _layernorm(ALPHA * h_ref[0] + m, g_ref[...], b_ref[...])


def _attn_out(o, w_o, h, g, b):
    B, lp, D = h.shape
    nt = lp // TM
    full2 = lambda bb, i: (0, 0)
    tile = pl.BlockSpec((1, TM, D), lambda bb, i: (bb, i, 0))
    return pl.pallas_call(
        _attn_out_kernel,
        grid=(B, nt),
        in_specs=[tile, pl.BlockSpec((D, D), full2), tile, pl.BlockSpec((1, D), full2), pl.BlockSpec((1, D), full2)],
        out_specs=tile,
        out_shape=jax.ShapeDtypeStruct((B, lp, D), jnp.float32),
        compiler_params=_cparams(("parallel", "parallel")),
        name="attn_out_ln",
    )(o, w_o.astype(jnp.bfloat16), h, g.reshape(1, D), b.reshape(1, D))


def kernel(x, meta_tokens, conv_w_pw1, conv_b_pw1, conv_w_dw, conv_b_dw, conv_ln_g, conv_ln_b, conv_w_pw2, conv_b_pw2, kv_w_k, kv_w_v, attn_w_q, attn_lam_q1, attn_lam_k1, attn_lam_q2, attn_lam_k2, attn_subln_g, attn_w_o, post_ln_g, post_ln_b, moe_w_group, moe_b_group, moe_w_expert, moe_b_expert, moe_w1, moe_w3, moe_w2):
    B, S, D = x.shape
    assert D == D_MODEL and S % TM == 0 and S % CHUNK == 0
    assert conv_w_pw1.shape[0] == 1 and attn_w_q.shape[0] == 1 and post_ln_g.shape[0] == DEPTH
    lp = S + TM

    def moe(h, l, drop_front):
        out = _moe_layer(h.reshape(B * lp, D), moe_w_group[l], moe_b_group[l], moe_w_expert[l],
                         moe_b_expert[l], moe_w1[l], moe_w3[l], moe_w2[l],
                         post_ln_g[l, 1], post_ln_b[l, 1], B, drop_front)
        return out.reshape(B, -1, D)

    h = _conv_layer(x, meta_tokens, conv_w_pw1[0], conv_b_pw1[0], conv_w_dw[0], conv_b_dw[0],
                    conv_ln_g[0], conv_ln_b[0], conv_w_pw2[0], conv_b_pw2[0],
                    post_ln_g[0, 0], post_ln_b[0, 0])
    h = moe(h, 0, False)
    q5, k5, v5 = _qkv(h, attn_w_q[0], kv_w_k, kv_w_v)
    lam_init = 0.8 - 0.6 * math.exp(-0.3 * 1)
    lam_params = jnp.stack([attn_lam_q1[0], attn_lam_k1[0], attn_lam_q2[0], attn_lam_k2[0]])
    o = _attention(q5, k5, v5, lam_params, attn_subln_g[0], lam_init)
    h = _attn_out(o, attn_w_o[0], h, post_ln_g[1, 0], post_ln_b[1, 0])
    return moe(h, 1, True)
```

```python
import functools
import math

import jax
import jax.numpy as jnp
from jax import lax
from jax.experimental import pallas as pl
from jax.experimental.pallas import tpu as pltpu

D_MODEL = 1024
N_META = 16
CHUNK = 64
DEPTH = 2
ALPHA = (2.0 * DEPTH) ** 0.25
CONV_W = 31
N_HEADS = 8
HEAD_DIM = 64
V_DIM = 128
ROT_DIM = 16
ROPE_THETA = 500000.0
N_GROUPS = 4
EXPERTS_PER_GROUP = 4
N_EXPERTS = 16
D_EXPERT = 512
EPS = 1e-5

TM = 256
PAD0 = TM - N_META
HALO = 32
CONV_SLACK = 8
TG = 256
ROW_TILE = 8
MOVE_ROWS = 512
LANES = 128
NEG = -1e30
VMEM_LIMIT = 48 * 1024 * 1024


def _cparams(sem):
    return pltpu.CompilerParams(dimension_semantics=sem, vmem_limit_bytes=VMEM_LIMIT)


def _layernorm(x, g, b):
    mu = jnp.mean(x, axis=-1, keepdims=True)
    xc = x - mu
    var = jnp.mean(xc * xc, axis=-1, keepdims=True)
    return xc * lax.rsqrt(var + EPS) * g + b


def _sigmoid(x):
    return 1.0 / (1.0 + jnp.exp(-x))


def _front_tile(meta_ref):
    return jnp.concatenate([jnp.zeros((PAD0, D_MODEL), jnp.float32), meta_ref[...]], axis=0)


def _stream_tile(i, x_ref, meta_ref):
    return jnp.where(i == 0, _front_tile(meta_ref), x_ref[0])


def _conv_in_kernel(x_ref, meta_ref, w_ref, b_ref, u_ref):
    i = pl.program_id(1)
    xt = _stream_tile(i, x_ref, meta_ref)
    h = jnp.dot(xt.astype(jnp.bfloat16), w_ref[...], preferred_element_type=jnp.float32) + b_ref[...]
    u = h[:, :D_MODEL] * _sigmoid(h[:, D_MODEL:])
    row = lax.broadcasted_iota(jnp.int32, (TM, 1), 0)
    u_ref[0] = jnp.where((i == 0) & (row < PAD0), 0.0, u)


def _conv_mid_kernel(ucur_ref, uprev_ref, x_ref, meta_ref, wdw_ref, bdw_ref, lng_ref, lnb_ref,
                     w2_ref, b2_ref, pg_ref, pb_ref, h_ref, win_ref, part_ref, conv_ref):
    i = pl.program_id(1)
    win_ref[0:HALO, :] = jnp.where(i == 0, 0.0, uprev_ref[0])
    win_ref[HALO:HALO + TM, :] = ucur_ref[0]
    win_ref[HALO + TM:, :] = jnp.zeros((CONV_SLACK, D_MODEL), jnp.float32)
    off = HALO - (CONV_W - 1)
    rows = TM + 16
    for c in range(D_MODEL // LANES):
        cs = slice(c * LANES, (c + 1) * LANES)
        acc = jnp.zeros((TM, LANES), jnp.float32)
        for r in range(8):
            part = None
            for j in range(r, CONV_W, 8):
                term = win_ref[j - r:j - r + rows, cs] * wdw_ref[j:j + 1, cs]
                part = term if part is None else part + term
            slot = (c * 8 + r) % 2
            part_ref[slot] = part
            acc = acc + part_ref[slot, off + r:off + r + TM, :]
        conv_ref[:, cs] = acc
    y = conv_ref[...] + bdw_ref[...]
    y = _layernorm(y, lng_ref[...], lnb_ref[...])
    y = y * _sigmoid(y)
    m = jnp.dot(y.astype(jnp.bfloat16), w2_ref[...], preferred_element_type=jnp.float32) + b2_ref[...]
    h0 = _stream_tile(i, x_ref, meta_ref)
    h_ref[0] = _layernorm(ALPHA * h0 + m, pg_ref[...], pb_ref[...])


def _conv_layer(x, meta, w1, b1, wdw, bdw, lng, lnb, w2, b2, pg, pb):
    B, S, D = x.shape
    nt = S // TM + 1
    lp = nt * TM
    xmap = lambda b, i: (b, jnp.maximum(i - 1, 0), 0)
    full2 = lambda b, i: (0, 0)
    u = pl.pallas_call(
        _conv_in_kernel,
        grid=(B, nt),
        in_specs=[
            pl.BlockSpec((1, TM, D), xmap),
            pl.BlockSpec((N_META, D), full2),
            pl.BlockSpec((D, 2 * D), full2),
            pl.BlockSpec((1, 2 * D), full2),
        ],
        out_specs=pl.BlockSpec((1, TM, D), lambda b, i: (b, i, 0)),
        out_shape=jax.ShapeDtypeStruct((B, lp, D), jnp.float32),
        compiler_params=_cparams(("parallel", "parallel")),
        name="conv_in",
    )(x, meta, w1.astype(jnp.bfloat16), b1.reshape(1, -1))
    wdw_p = jnp.concatenate([wdw, jnp.zeros((HALO - CONV_W, D), wdw.dtype)], axis=0)
    vec = lambda a: a.reshape(1, D)
    h1 = pl.pallas_call(
        _conv_mid_kernel,
        grid=(B, nt),
        in_specs=[
            pl.BlockSpec((1, TM, D), lambda b, i: (b, i, 0)),
            pl.BlockSpec((1, HALO, D), lambda b, i: (b, jnp.maximum(i * (TM // HALO) - 1, 0), 0)),
            pl.BlockSpec((1, TM, D), xmap),
            pl.BlockSpec((N_META, D), full2),
            pl.BlockSpec((HALO, D), full2),
            pl.BlockSpec((1, D), full2),
            pl.BlockSpec((1, D), full2),
            pl.BlockSpec((1, D), full2),
            pl.BlockSpec((D, D), full2),
            pl.BlockSpec((1, D), full2),
            pl.BlockSpec((1, D), full2),
            pl.BlockSpec((1, D), full2),
        ],
        out_specs=pl.BlockSpec((1, TM, D), lambda b, i: (b, i, 0)),
        out_shape=jax.ShapeDtypeStruct((B, lp, D), jnp.float32),
        scratch_shapes=[
            pltpu.VMEM((HALO + TM + CONV_SLACK, D), jnp.float32),
            pltpu.VMEM((2, TM + 16, LANES), jnp.float32),
            pltpu.VMEM((TM, D), jnp.float32),
        ],
        compiler_params=_cparams(("parallel", "parallel")),
        name="conv_mid",
    )(u, u, x, meta, wdw_p, vec(bdw), vec(lng), vec(lnb), w2.astype(jnp.bfloat16), vec(b2), vec(pg), vec(pb))
    return h1


def _route_kernel(h_ref, whi_ref, wlo_ref, b_ref, info_ref, cnt_ref, htile_ref, carry_ref):
    step = pl.program_id(0)

    @pl.when(step == 0)
    def _():
        carry_ref[...] = jnp.zeros_like(carry_ref)

    h = h_ref[...]
    _to_token_tiles(htile_ref, h)
    hhi = h.astype(jnp.bfloat16)
    hlo = (h - hhi.astype(jnp.float32)).astype(jnp.bfloat16)
    lg = (jnp.dot(hhi, whi_ref[...], preferred_element_type=jnp.float32)
          + jnp.dot(hlo, whi_ref[...], preferred_element_type=jnp.float32)
          + jnp.dot(hhi, wlo_ref[...], preferred_element_type=jnp.float32)) + b_ref[...]
    lane_i = lax.broadcasted_iota(jnp.int32, (TM, LANES), 1)
    lane = lane_i.astype(jnp.float32)

    def first_argmax(v, vmax):
        return jnp.min(jnp.where(v == vmax, lane, float(LANES)), axis=1, keepdims=True)

    gmask = lane_i < N_GROUPS
    glog = jnp.where(gmask, lg, NEG)
    gmax = jnp.max(glog, axis=1, keepdims=True)
    gidx = first_argmax(glog, gmax)
    gsum = jnp.sum(jnp.where(gmask, jnp.exp(glog - gmax), 0.0), axis=1, keepdims=True)
    g_w = 1.0 / gsum
    elane = lane_i - N_GROUPS
    egrp = (elane >> 2).astype(jnp.float32)
    emask = (elane >= 0) & (elane < N_EXPERTS) & (egrp == gidx)
    elog = jnp.where(emask, lg, NEG)
    e1 = jnp.max(elog, axis=1, keepdims=True)
    i1 = first_argmax(elog, e1)
    elog2 = jnp.where(lane == i1, NEG, elog)
    e2 = jnp.max(elog2, axis=1, keepdims=True)
    i2 = first_argmax(elog2, e2)
    d = jnp.exp(e2 - e1)
    w1 = g_w / (1.0 + d)
    w2 = g_w * d / (1.0 + d)
    x1 = i1 - float(N_GROUPS)
    x2 = i2 - float(N_GROUPS)
    oh1 = jnp.where(lane == x1, 1.0, 0.0)
    oh2 = jnp.where(lane == x2, 1.0, 0.0)
    r_i = lax.broadcasted_iota(jnp.int32, (TM, TM), 0)
    c_i = lax.broadcasted_iota(jnp.int32, (TM, TM), 1)
    ltri = jnp.where(c_i < r_i, 1.0, 0.0).astype(jnp.bfloat16)
    pre1 = jnp.dot(ltri, oh1.astype(jnp.bfloat16), preferred_element_type=jnp.float32)
    pre2 = jnp.dot(ltri, oh2.astype(jnp.bfloat16), preferred_element_type=jnp.float32)
    cnt1 = jnp.sum(oh1, axis=0, keepdims=True)
    cnt2 = jnp.sum(oh2, axis=0, keepdims=True)
    carry = carry_ref[0:1, :]
    rank1 = jnp.sum(oh1 * (carry + pre1), axis=1, keepdims=True)
    rank2 = jnp.sum(oh2 * (carry + cnt1 + pre2), axis=1, keepdims=True)
    new_carry = carry + cnt1 + cnt2
    carry_ref[...] = jnp.broadcast_to(new_carry, carry_ref.shape)
    cnt_ref[...] = jnp.broadcast_to(new_carry, cnt_ref.shape)
    info = jnp.where(lane_i == 0, x1, 0.0)
    info = jnp.where(lane_i == 1, x2, info)
    info = jnp.where(lane_i == 2, w1, info)
    info = jnp.where(lane_i == 3, w2, info)
    info = jnp.where(lane_i == 4, rank1, info)
    info = jnp.where(lane_i == 5, rank2, info)
    info_ref[...] = info


def _route(h2d, w_group, b_group, w_expert, b_expert):
    T, D = h2d.shape
    wr = jnp.zeros((D, LANES), jnp.float32)
    wr = wr.at[:, :N_GROUPS].set(w_group).at[:, N_GROUPS:N_GROUPS + N_EXPERTS].set(w_expert)
    br = jnp.zeros((1, LANES), jnp.float32)
    br = br.at[0, :N_GROUPS].set(b_group).at[0, N_GROUPS:N_GROUPS + N_EXPERTS].set(b_expert)
    whi = wr.astype(jnp.bfloat16)
    wlo = (wr - whi.astype(jnp.float32)).astype(jnp.bfloat16)
    return pl.pallas_call(
        _route_kernel,
        grid=(T // TM,),
        in_specs=[
            pl.BlockSpec((TM, D), lambda i: (i, 0)),
            pl.BlockSpec((D, LANES), lambda i: (0, 0)),
            pl.BlockSpec((D, LANES), lambda i: (0, 0)),
            pl.BlockSpec((1, LANES), lambda i: (0, 0)),
        ],
        out_specs=[
            pl.BlockSpec((TM, LANES), lambda i: (i, 0)),
            pl.BlockSpec((8, LANES), lambda i: (0, 0)),
            pl.BlockSpec((TM * ROW_TILE, LANES), lambda i: (i, 0)),
        ],
        out_shape=[
            jax.ShapeDtypeStruct((T, LANES), jnp.float32),
            jax.ShapeDtypeStruct((8, LANES), jnp.float32),
            jax.ShapeDtypeStruct((T * ROW_TILE, LANES), jnp.float32),
        ],
        scratch_shapes=[pltpu.VMEM((8, LANES), jnp.float32)],
        compiler_params=_cparams(("arbitrary",)),
        name="route",
    )(h2d, whi, wlo, br)


def _dispatch_plan(info, cnt, n_rows):
    ex = info[:, 0:2].astype(jnp.int32)
    rank = info[:, 4:6].astype(jnp.int32)
    counts = cnt[0, :N_EXPERTS].astype(jnp.int32)
    padded = ((counts + TG - 1) // TG) * TG
    ends = jnp.cumsum(padded)
    starts = ends - padded
    pos = starts[ex] + rank
    i = jnp.arange(TG, dtype=jnp.int32)[None, :]
    pad_rows = jnp.where(i < (padded - counts)[:, None], (starts + counts)[:, None] + i, -1)
    tile_start = jnp.arange(n_rows // TG, dtype=jnp.int32) * TG
    tile_expert = jnp.minimum(
        jnp.sum((tile_start[:, None] >= ends[None, :]).astype(jnp.int32), axis=1), N_EXPERTS - 1)
    n_used = (ends[-1] // TG).astype(jnp.int32).reshape(1)
    return pos, pad_rows, tile_expert.astype(jnp.int32), n_used


def _to_token_tiles(ref, x):
    rows = x.shape[0]
    for j in range(ROW_TILE):
        ref[pl.ds(j, rows, stride=ROW_TILE), :] = x[:, j * LANES:(j + 1) * LANES]


def _from_token_tiles(ref, rows):
    return jnp.concatenate([ref[pl.ds(j, rows, stride=ROW_TILE), :] for j in range(ROW_TILE)], axis=1)


def _issue_row_copies(copy_pair, n_pairs):
    def start(i, c):
        a, b = copy_pair(i)
        a.start(priority=0)
        b.start(priority=1)
        return c

    def wait(i, c):
        a, b = copy_pair(i)
        a.wait()
        b.wait()
        return c

    lax.fori_loop(0, n_pairs, start, 0, unroll=4)
    lax.fori_loop(0, n_pairs, wait, 0, unroll=4)


def _dispatch_kernel(n_tiles, pads_per_step, tails_per_step, dst_ref, pad_ref, nu_ref, table_ref,
                     out_ref, zrow_ref, ztile_ref, sem, zsem):
    step = pl.program_id(0)

    @pl.when(step == 0)
    def _():
        zrow_ref[...] = jnp.zeros_like(zrow_ref)
        ztile_ref[...] = jnp.zeros_like(ztile_ref)

    def row_copy(i, k):
        return pltpu.make_async_copy(table_ref.at[i], out_ref.at[dst_ref[0, 0, 2 * i + k]], sem)

    _issue_row_copies(lambda i: (row_copy(i, 0), row_copy(i, 1)), MOVE_ROWS // 2)

    def tail_copy(k):
        tail = nu_ref[0] + step * tails_per_step + k
        copy = pltpu.make_async_copy(ztile_ref, out_ref.at[pl.ds(jnp.minimum(tail, n_tiles - 1) * TG, TG)], zsem)
        return tail < n_tiles, copy

    def pad_copy(r):
        return pltpu.make_async_copy(zrow_ref, out_ref.at[jnp.maximum(pad_ref[0, 0, r], 0)], zsem)

    def start(r, c):
        @pl.when(pad_ref[0, 0, r] >= 0)
        def _():
            pad_copy(r).start()
        return c

    def wait(r, c):
        @pl.when(pad_ref[0, 0, r] >= 0)
        def _():
            pad_copy(r).wait()
        return c

    for k in range(tails_per_step):
        live, copy = tail_copy(k)
        pl.when(live)(copy.start)
    lax.fori_loop(0, pads_per_step, start, 0)
    lax.fori_loop(0, pads_per_step, wait, 0)
    for k in range(tails_per_step):
        live, copy = tail_copy(k)
        pl.when(live)(copy.wait)


def _dispatch_rows(table, dst, pad_rows, n_used, n_out):
    n = dst.size
    steps = n // MOVE_ROWS
    pads_per_step = pl.cdiv(pad_rows.size, steps)
    tails_per_step = pl.cdiv(n_out // TG - n // TG, steps)
    pads = jnp.concatenate([pad_rows.reshape(-1),
                            jnp.full((steps * pads_per_step - pad_rows.size,), -1, jnp.int32)])
    return pl.pallas_call(
        functools.partial(_dispatch_kernel, n_out // TG, pads_per_step, tails_per_step),
        grid=(steps,),
        in_specs=[
            pl.BlockSpec((1, 1, MOVE_ROWS), lambda i: (i, 0, 0), memory_space=pltpu.SMEM),
            pl.BlockSpec((1, 1, pads_per_step), lambda i: (i, 0, 0), memory_space=pltpu.SMEM),
            pl.BlockSpec(memory_space=pltpu.SMEM),
            pl.BlockSpec((MOVE_ROWS // 2,) + table.shape[1:], lambda i: (i, 0, 0)),
        ],
        out_specs=pl.BlockSpec(memory_space=pl.ANY),
        out_shape=jax.ShapeDtypeStruct((n_out,) + table.shape[1:], table.dtype),
        scratch_shapes=[
            pltpu.VMEM(table.shape[1:], table.dtype),
            pltpu.VMEM((TG,) + table.shape[1:], table.dtype),
            pltpu.SemaphoreType.DMA(()),
            pltpu.SemaphoreType.DMA(()),
        ],
        compiler_params=_cparams(("arbitrary",)),
        name="dispatch_rows",
    )(dst.reshape(steps, 1, MOVE_ROWS), pads.reshape(steps, 1, pads_per_step), n_used, table)


def _gmm_kernel(te_ref, nu_ref, x_ref, w1_ref, w3_ref, w2_ref, y_ref, w1b_ref, w3b_ref, w2b_ref):
    j = pl.program_id(0)

    @pl.when((j == 0) | (te_ref[j] != te_ref[jnp.maximum(j - 1, 0)]))
    def _():
        w1b_ref[...] = w1_ref[0].astype(jnp.bfloat16)
        w3b_ref[...] = w3_ref[0].astype(jnp.bfloat16)
        w2b_ref[...] = w2_ref[0].astype(jnp.bfloat16)

    @pl.when(j < nu_ref[0])
    def _():
        x = _from_token_tiles(x_ref, TG).astype(jnp.bfloat16)
        a = jnp.dot(x, w1b_ref[...], preferred_element_type=jnp.float32)
        g = jnp.dot(x, w3b_ref[...], preferred_element_type=jnp.float32)
        hh = (a * _sigmoid(a) * g).astype(jnp.bfloat16)
        _to_token_tiles(y_ref, jnp.dot(hh, w2b_ref[...], preferred_element_type=jnp.float32))

    @pl.when(j >= nu_ref[0])
    def _():
        y_ref[...] = jnp.zeros_like(y_ref)


def _gmm(xs2d, n_rows, w1, w3, w2, tile_expert, n_used):
    D, de = w1.shape[1], w1.shape[2]
    blk = pl.BlockSpec((TG * ROW_TILE, LANES), lambda j, te, nu: (j, 0))
    grid_spec = pltpu.PrefetchScalarGridSpec(
        num_scalar_prefetch=2,
        grid=(n_rows // TG,),
        in_specs=[
            blk,
            pl.BlockSpec((1, D, de), lambda j, te, nu: (te[j], 0, 0)),
            pl.BlockSpec((1, D, de), lambda j, te, nu: (te[j], 0, 0)),
            pl.BlockSpec((1, de, D), lambda j, te, nu: (te[j], 0, 0)),
        ],
        out_specs=blk,
        scratch_shapes=[pltpu.VMEM((D, de), jnp.bfloat16), pltpu.VMEM((D, de), jnp.bfloat16),
                        pltpu.VMEM((de, D), jnp.bfloat16)],
    )
    return pl.pallas_call(
        _gmm_kernel,
        grid_spec=grid_spec,
        out_shape=jax.ShapeDtypeStruct((n_rows * ROW_TILE, LANES), jnp.float32),
        compiler_params=_cparams(("arbitrary",)),
        name="expert_mlp",
    )(tile_expert, n_used, xs2d, w1, w3, w2)


def _combine_kernel(n_steps, idx_ref, idx_next_ref, h_ref, info_ref, g_ref, b_ref, ys_ref, o_ref, ybuf_ref, sem):
    step = pl.program_id(0)

    def gather(idx, slot):
        def row_copy(r):
            return pltpu.make_async_copy(ys_ref.at[idx[0, 0, r]],
                                         ybuf_ref.at[slot, pl.ds(r * ROW_TILE, ROW_TILE), :], sem.at[slot])
        return lambda i: (row_copy(2 * i), row_copy(2 * i + 1))

    def start(copy_pair):
        def body(i, c):
            first, second = copy_pair(i)
            first.start(priority=0)
            second.start(priority=1)
            return c
        lax.fori_loop(0, TM, body, 0, unroll=4)

    def wait(copy_pair):
        def body(i, c):
            first, second = copy_pair(i)
            first.wait()
            second.wait()
            return c
        lax.fori_loop(0, TM, body, 0, unroll=4)

    @pl.when(step == 0)
    def _():
        start(gather(idx_ref, 0))

    for slot in range(2):
        @pl.when((step & 1) == slot)
        def _():
            @pl.when(step + 1 < n_steps)
            def _():
                start(gather(idx_next_ref, 1 - slot))

            wait(gather(idx_ref, slot))
            rows = ybuf_ref.at[slot]
            y0 = jnp.concatenate([rows[pl.ds(j, TM, stride=ROW_TILE), :] for j in range(ROW_TILE)], axis=1)
            y1 = jnp.concatenate([rows[pl.ds(TM * ROW_TILE + j, TM, stride=ROW_TILE), :]
                                  for j in range(ROW_TILE)], axis=1)
            info = info_ref[...]
            f = info[:, 2:3] * y0 + info[:, 3:4] * y1
            o_ref[...] = _layernorm(ALPHA * h_ref[...] + f, g_ref[...], b_ref[...])


def _combine_ln(h2d, ys, pos, info, g, b, batch, drop_front):
    T, D = h2d.shape
    nt = T // TM
    ntb = nt // batch
    if drop_front:
        n_steps = batch * (ntb - 1)
        tile = lambda s: s + s // (ntb - 1) + 1
    else:
        n_steps = nt
        tile = lambda s: s
    idx = jnp.concatenate([pos[:, 0].reshape(nt, TM), pos[:, 1].reshape(nt, TM)], axis=1).reshape(nt, 1, 2 * TM)
    idx_spec = lambda f: pl.BlockSpec((1, 1, 2 * TM), lambda s: (f(s), 0, 0), memory_space=pltpu.SMEM)
    return pl.pallas_call(
        functools.partial(_combine_kernel, n_steps),
        grid=(n_steps,),
        in_specs=[
            idx_spec(tile),
            idx_spec(lambda s: tile(jnp.minimum(s + 1, n_steps - 1))),
            pl.BlockSpec((TM, D), lambda s: (tile(s), 0)),
            pl.BlockSpec((TM, LANES), lambda s: (tile(s), 0)),
            pl.BlockSpec((1, D), lambda s: (0, 0)),
            pl.BlockSpec((1, D), lambda s: (0, 0)),
            pl.BlockSpec(memory_space=pl.ANY),
        ],
        out_specs=pl.BlockSpec((TM, D), lambda s: (s, 0)),
        out_shape=jax.ShapeDtypeStruct((n_steps * TM, D), jnp.float32),
        scratch_shapes=[pltpu.VMEM((2, 2 * TM * ROW_TILE, LANES), jnp.float32), pltpu.SemaphoreType.DMA((2,))],
        compiler_params=_cparams(("arbitrary",)),
        name="moe_combine_ln",
    )(idx, idx, h2d, info, g.reshape(1, D), b.reshape(1, D), ys)


def _moe_layer(h2d, wg, bg, we, be, w1, w3, w2, pg, pb, batch, drop_front):
    T, D = h2d.shape
    n_rows = 2 * T + N_EXPERTS * TG
    info, cnt, h_tiles = _route(h2d, wg, bg, we, be)
    pos, pad_rows, tile_expert, n_used = _dispatch_plan(info, cnt, n_rows)
    xs = _dispatch_rows(h_tiles.reshape(T, ROW_TILE, LANES), pos, pad_rows, n_used, n_rows)
    ys = _gmm(xs.reshape(-1, LANES), n_rows, w1, w3, w2, tile_expert, n_used)
    return _combine_ln(h2d, ys.reshape(n_rows, ROW_TILE, LANES), pos, info, pg, pb, batch, drop_front)


def _qkv_kernel(h_ref, wq_ref, wk_ref, wv_ref, cosT_ref, sinT_ref, cn_ref, s1_ref, s2_ref,
                q_ref, k_ref, v_ref):
    hb = h_ref[0].astype(jnp.bfloat16)
    half = ROT_DIM // 2
    nt_dims = (((1,), (1,)), ((), ()))
    qT = lax.dot_general(wq_ref[...], hb, nt_dims, preferred_element_type=jnp.float32)
    cosT = cosT_ref[...]
    sinT = sinT_ref[...]
    scale = HEAD_DIM ** -0.5 * math.log2(math.e)
    for hh in range(N_HEADS):
        parts = []
        for c in range(2):
            r0 = hh * 2 * HEAD_DIM + c * HEAD_DIM
            t1 = qT[r0:r0 + half]
            t2 = qT[r0 + half:r0 + 2 * half]
            parts += [t1 * cosT - t2 * sinT, t1 * sinT + t2 * cosT, qT[r0 + 2 * half:r0 + HEAD_DIM]]
        q_ref[0, hh, 0] = (jnp.concatenate(parts, axis=0) * scale).astype(jnp.bfloat16)
    k = jnp.dot(hb, wk_ref[...], preferred_element_type=jnp.float32)
    cn = cn_ref[...]
    s1 = s1_ref[...]
    s2 = s2_ref[...]
    for hh in range(N_HEADS):
        kb = k[:, hh * LANES:(hh + 1) * LANES]
        kr = kb * cn + pltpu.roll(kb, LANES - half, 1) * s1 + pltpu.roll(kb, half, 1) * s2
        k_ref[0, hh, 0] = kr.astype(jnp.bfloat16)
    vT = lax.dot_general(wv_ref[...], hb, nt_dims, preferred_element_type=jnp.float32)
    extra = lax.broadcasted_iota(jnp.int32, (V_ROWS - V_DIM, TM), 0)
    ones_row = jnp.where(extra == 0, 1.0, 0.0).astype(jnp.bfloat16)
    for hh in range(N_HEADS):
        v_ref[0, hh, 0, 0:V_DIM, :] = vT[hh * V_DIM:(hh + 1) * V_DIM].astype(jnp.bfloat16)
        v_ref[0, hh, 0, V_DIM:V_ROWS, :] = ones_row


def _rope_tables(lp):
    half = ROT_DIM // 2
    pos = jnp.arange(lp, dtype=jnp.float32) - PAD0
    inv = ROPE_THETA ** (-jnp.arange(0, ROT_DIM, 2, dtype=jnp.float32) / ROT_DIM)
    ang = pos[:, None] * inv[None, :]
    cos, sin = jnp.cos(ang), jnp.sin(ang)
    lane = jnp.arange(LANES) % HEAD_DIM
    is1 = lane < half
    is2 = (lane >= half) & (lane < ROT_DIM)
    fi = jnp.where(is1, lane, jnp.where(is2, lane - half, 0))
    cn = jnp.where(is1 | is2, cos[:, fi], 1.0)
    s1 = jnp.where(is1, -sin[:, fi], 0.0)
    s2 = jnp.where(is2, sin[:, fi], 0.0)
    return cos.T, sin.T, cn, s1, s2


def _qkv(h, wq, wk, wv):
    B, lp, D = h.shape
    nt = lp // TM
    cosT, sinT, cn, s1, s2 = _rope_tables(lp)
    full2 = lambda b, i: (0, 0)
    hd2 = 2 * HEAD_DIM
    out5 = lambda r, c: jax.ShapeDtypeStruct((B, N_HEADS, nt, r, c), jnp.bfloat16)
    spec5 = lambda r, c: pl.BlockSpec((1, N_HEADS, 1, r, c), lambda b, i: (b, 0, i, 0, 0))
    return pl.pallas_call(
        _qkv_kernel,
        grid=(B, nt),
        in_specs=[
            pl.BlockSpec((1, TM, D), lambda b, i: (b, i, 0)),
            pl.BlockSpec((D, D), full2),
            pl.BlockSpec((D, D), full2),
            pl.BlockSpec((D, D), full2),
            pl.BlockSpec((ROT_DIM // 2, TM), lambda b, i: (0, i)),
            pl.BlockSpec((ROT_DIM // 2, TM), lambda b, i: (0, i)),
            pl.BlockSpec((TM, LANES), lambda b, i: (i, 0)),
            pl.BlockSpec((TM, LANES), lambda b, i: (i, 0)),
            pl.BlockSpec((TM, LANES), lambda b, i: (i, 0)),
        ],
        out_specs=[spec5(hd2, TM), spec5(TM, hd2), spec5(V_ROWS, TM)],
        out_shape=[out5(hd2, TM), out5(TM, hd2), out5(V_ROWS, TM)],
        compiler_params=_cparams(("parallel", "parallel")),
        name="qkv_proj",
    )(h, wq.T.astype(jnp.bfloat16), wk.astype(jnp.bfloat16), wv.T.astype(jnp.bfloat16), cosT, sinT, cn, s1, s2)


MASK_PAD, MASK_DIAG, MASK_ALL = 1, 2, 4
ATT_UNROLL = 8
SOFTMAX_ROWS = 32


MASK_VALUE = -1e30
V_ROWS = V_DIM + 16
N_CHUNK_T = TM // CHUNK


def _attn_mask_tables():
    krow = jnp.arange(TM)[:, None]
    col = jnp.arange(LANES)[None, :]
    diag = (col < N_CHUNK_T) & (krow // CHUNK == col)
    pad = (col == N_CHUNK_T) & (krow < PAD0)
    everything = jnp.broadcast_to(col == N_CHUNK_T + 1, (TM, LANES))
    none = jnp.zeros((TM, LANES), bool)
    sel = jnp.stack([none, pad, diag, pad | diag, everything]).astype(jnp.bfloat16)
    row = jnp.arange(LANES)[:, None]
    qchunk = (jnp.arange(2 * TM)[None, :] % TM) // CHUNK
    hidden = ((row < N_CHUNK_T) & (row > qchunk)) | (row == N_CHUNK_T) | (row == N_CHUNK_T + 1)
    val = jnp.where(hidden, MASK_VALUE, 0.0).astype(jnp.bfloat16)
    return sel, val


def _attn_kernel(lam_init, q_ref, k_ref, v_ref, sel_ref, val_ref, lam_ref, g_ref, o_ref,
                 qcat_ref, s_ref, mt_ref, p_ref, alpha_ref, m_ref, acc_ref):
    qi = pl.program_id(2)
    W = 2 * TM
    F = 2 * HEAD_DIM
    q = q_ref[0, 0, 0]
    zero = jnp.zeros((HEAD_DIM, TM), jnp.bfloat16)
    qcat_ref[0:F, 0:TM] = jnp.concatenate([q[:HEAD_DIM], zero], axis=0)
    qcat_ref[0:F, TM:W] = jnp.concatenate([zero, q[HEAD_DIM:]], axis=0)
    qcat_ref[F:2 * F, :] = val_ref[...]
    s_ref[1] = jnp.full((TM, W), -jnp.inf, jnp.float32)
    p_ref[0] = jnp.zeros((TM, W), jnp.bfloat16)
    alpha_ref[0] = jnp.ones((8, W), jnp.float32)
    mt_ref[1] = jnp.full((8, W), -jnp.inf, jnp.float32)
    m_ref[...] = jnp.full(m_ref.shape, jnp.finfo(jnp.float32).min, jnp.float32)
    acc_ref[...] = jnp.zeros_like(acc_ref)

    def stages(t, cur):
        prv = 1 - cur
        kind = jnp.where(t > qi, MASK_ALL, jnp.where(t == 0, MASK_PAD, 0) + jnp.where(t == qi, MASK_DIAG, 0))
        lhs = jnp.concatenate([k_ref[0, 0, jnp.minimum(t, qi)], sel_ref[kind]], axis=1)
        s_new = jnp.dot(lhs, qcat_ref[...], preferred_element_type=jnp.float32)
        s_ref[cur] = s_new
        mt_ref[cur, 0:1, :] = jnp.max(s_new, axis=0, keepdims=True)
        vt = v_ref[0, 0, jnp.clip(t - 2, 0, qi)]
        pv = jnp.dot(vt, p_ref[cur], preferred_element_type=jnp.float32)
        acc_ref[...] = alpha_ref[cur, 0:1, :] * acc_ref[...] + pv
        m_old = m_ref[0:1, :]
        m_new = jnp.maximum(m_old, mt_ref[prv, 0:1, :])
        for r in range(0, TM, SOFTMAX_ROWS):
            p = jnp.exp2(s_ref[prv, r:r + SOFTMAX_ROWS, :] - m_new)
            p_ref[prv, r:r + SOFTMAX_ROWS, :] = p.astype(jnp.bfloat16)
        m_ref[0:1, :] = m_new
        alpha_ref[prv, 0:1, :] = jnp.exp2(m_old - m_new)

    def unrolled(first, unroll):
        def body(i, carry):
            for u in range(unroll):
                stages(first + i * unroll + u, u & 1)
            return carry
        return body

    n_iter = qi + 3
    n_main = n_iter // ATT_UNROLL
    lax.fori_loop(0, n_main, unrolled(0, ATT_UNROLL), 0)
    lax.fori_loop(0, (n_iter - n_main * ATT_UNROLL + 1) // 2, unrolled(n_main * ATT_UNROLL, 2), 0)

    lp = lam_ref[...]
    lam = (jnp.exp(jnp.sum(lp[0:1] * lp[1:2], axis=1, keepdims=True))
           - jnp.exp(jnp.sum(lp[2:3] * lp[3:4], axis=1, keepdims=True)) + lam_init)
    l = acc_ref[V_DIM:V_DIM + 1, :]
    l = jnp.where(l == 0.0, 1.0, l)
    acc = acc_ref[0:V_DIM, :]
    o = acc[:, 0:TM] / l[:, 0:TM] - lam * (acc[:, TM:W] / l[:, TM:W])
    ms = jnp.mean(o * o, axis=0, keepdims=True)
    y = o * lax.rsqrt(ms + EPS) * g_ref[...] * (1.0 - lam_init)
    o_ref[0] = y.T.astype(o_ref.dtype)


def _attention(q5, k5, v5, lam_params, subln_g, lam_init):
    B, H, nt, _, _ = q5.shape
    lp = nt * TM
    kernel = functools.partial(_attn_kernel, lam_init)
    sel, val = _attn_mask_tables()
    return pl.pallas_call(
        kernel,
        grid=(B, H, nt),
        in_specs=[
            pl.BlockSpec((1, 1, 1, 2 * HEAD_DIM, TM), lambda b, h, i: (b, h, i, 0, 0)),
            pl.BlockSpec((1, 1, nt, TM, 2 * HEAD_DIM), lambda b, h, i: (b, h, 0, 0, 0)),
            pl.BlockSpec((1, 1, nt, V_ROWS, TM), lambda b, h, i: (b, h, 0, 0, 0)),
            pl.BlockSpec((5, TM, LANES), lambda b, h, i: (0, 0, 0)),
            pl.BlockSpec((LANES, 2 * TM), lambda b, h, i: (0, 0)),
            pl.BlockSpec((4, HEAD_DIM), lambda b, h, i: (0, 0)),
            pl.BlockSpec((V_DIM, 1), lambda b, h, i: (0, 0)),
        ],
        out_specs=pl.BlockSpec((1, TM, V_DIM), lambda b, h, i: (b, i, h)),
        out_shape=jax.ShapeDtypeStruct((B, lp, H * V_DIM), jnp.bfloat16),
        scratch_shapes=[
            pltpu.VMEM((4 * HEAD_DIM, 2 * TM), jnp.bfloat16),
            pltpu.VMEM((2, TM, 2 * TM), jnp.float32),
            pltpu.VMEM((2, 8, 2 * TM), jnp.float32),
            pltpu.VMEM((2, TM, 2 * TM), jnp.bfloat16),
            pltpu.VMEM((2, 8, 2 * TM), jnp.float32),
            pltpu.VMEM((8, 2 * TM), jnp.float32),
            pltpu.VMEM((V_ROWS, 2 * TM), jnp.float32),
        ],
        compiler_params=_cparams(("parallel", "parallel", "arbitrary")),
        name="diff_attention",
    )(q5, k5, v5, sel, val, lam_params, subln_g.reshape(V_DIM, 1))


def _attn_out_kernel(o_ref, w_ref, h_ref, g_ref, b_ref, out_ref):
    m = jnp.dot(o_ref[0], w_ref[...], preferred_element_type=jnp.float32)
    out_ref[0] = _layernorm(ALPHA * h_ref[0] + m, g_ref[...], b_ref[...])


def _attn_out(o, w_o, h, g, b):
    B, lp, D = h.shape
    nt = lp // TM
    full2 = lambda bb, i: (0, 0)
    tile = pl.BlockSpec((1, TM, D), lambda bb, i: (bb, i, 0))
    return pl.pallas_call(
        _attn_out_kernel,
        grid=(B, nt),
        in_specs=[tile, pl.BlockSpec((D, D), full2), tile, pl.BlockSpec((1, D), full2), pl.BlockSpec((1, D), full2)],
        out_specs=tile,
        out_shape=jax.ShapeDtypeStruct((B, lp, D), jnp.float32),
        compiler_params=_cparams(("parallel", "parallel")),
        name="attn_out_ln",
    )(o, w_o.astype(jnp.bfloat16), h, g.reshape(1, D), b.reshape(1, D))


def kernel(x, meta_tokens, conv_w_pw1, conv_b_pw1, conv_w_dw, conv_b_dw, conv_ln_g, conv_ln_b, conv_w_pw2, conv_b_pw2, kv_w_k, kv_w_v, attn_w_q, attn_lam_q1, attn_lam_k1, attn_lam_q2, attn_lam_k2, attn_subln_g, attn_w_o, post_ln_g, post_ln_b, moe_w_group, moe_b_group, moe_w_expert, moe_b_expert, moe_w1, moe_w3, moe_w2):
    B, S, D = x.shape
    assert D == D_MODEL and S % TM == 0 and S % CHUNK == 0
    assert conv_w_pw1.shape[0] == 1 and attn_w_q.shape[0] == 1 and post_ln_g.shape[0] == DEPTH
    lp = S + TM

    def moe(h, l, drop_front):
        out = _moe_layer(h.reshape(B * lp, D), moe_w_group[l], moe_b_group[l], moe_w_expert[l],
                         moe_b_expert[l], moe_w1[l], moe_w3[l], moe_w2[l],
                         post_ln_g[l, 1], post_ln_b[l, 1], B, drop_front)
        return out.reshape(B, -1, D)

    h = _conv_layer(x, meta_tokens, conv_w_pw1[0], conv_b_pw1[0], conv_w_dw[0], conv_b_dw[0],
                    conv_ln_g[0], conv_ln_b[0], conv_w_pw2[0], conv_b_pw2[0],
                    post_ln_g[0, 0], post_ln_b[0, 0])
    h = moe(h, 0, False)
    q5, k5, v5 = _qkv(h, attn_w_q[0], kv_w_k, kv_w_v)
    lam_init = 0.8 - 0.6 * math.exp(-0.3 * 1)
    lam_params = jnp.stack([attn_lam_q1[0], attn_lam_k1[0], attn_lam_q2[0], attn_lam_k2[0]])
    o = _attention(q5, k5, v5, lam_params, attn_subln_g[0], lam_init)
    h = _attn_out(o, attn_w_o[0], h, post_ln_g[1, 0], post_ln_b[1, 0])
    return moe(h, 1, True)
```

```python
import functools
import math

import jax
import jax.numpy as jnp
from jax import lax
from jax.experimental import pallas as pl
from jax.experimental.pallas import tpu as pltpu

D_MODEL = 1024
N_META = 16
CHUNK = 64
DEPTH = 2
ALPHA = (2.0 * DEPTH) ** 0.25
CONV_W = 31
N_HEADS = 8
HEAD_DIM = 64
V_DIM = 128
ROT_DIM = 16
ROPE_THETA = 500000.0
N_GROUPS = 4
EXPERTS_PER_GROUP = 4
N_EXPERTS = 16
D_EXPERT = 512
EPS = 1e-5

TM = 256
PAD0 = TM - N_META
HALO = 32
CONV_SLACK = 8
TG = 256
ROW_TILE = 8
MOVE_ROWS = 512
LANES = 128
NEG = -1e30
VMEM_LIMIT = 48 * 1024 * 1024


def _cparams(sem):
    return pltpu.CompilerParams(dimension_semantics=sem, vmem_limit_bytes=VMEM_LIMIT)


def _layernorm(x, g, b):
    mu = jnp.mean(x, axis=-1, keepdims=True)
    xc = x - mu
    var = jnp.mean(xc * xc, axis=-1, keepdims=True)
    return xc * lax.rsqrt(var + EPS) * g + b


def _sigmoid(x):
    return 1.0 / (1.0 + jnp.exp(-x))


def _front_tile(meta_ref):
    return jnp.concatenate([jnp.zeros((PAD0, D_MODEL), jnp.float32), meta_ref[...]], axis=0)


def _stream_tile(i, x_ref, meta_ref):
    return jnp.where(i == 0, _front_tile(meta_ref), x_ref[0])


def _conv_in_kernel(x_ref, meta_ref, w_ref, b_ref, u_ref):
    i = pl.program_id(1)
    xt = _stream_tile(i, x_ref, meta_ref)
    h = jnp.dot(xt.astype(jnp.bfloat16), w_ref[...], preferred_element_type=jnp.float32) + b_ref[...]
    u = h[:, :D_MODEL] * _sigmoid(h[:, D_MODEL:])
    row = lax.broadcasted_iota(jnp.int32, (TM, 1), 0)
    u_ref[0] = jnp.where((i == 0) & (row < PAD0), 0.0, u)


def _conv_mid_kernel(ucur_ref, uprev_ref, x_ref, meta_ref, wdw_ref, bdw_ref, lng_ref, lnb_ref,
                     w2_ref, b2_ref, pg_ref, pb_ref, h_ref, win_ref, part_ref, conv_ref):
    i = pl.program_id(1)
    win_ref[0:HALO, :] = jnp.where(i == 0, 0.0, uprev_ref[0])
    win_ref[HALO:HALO + TM, :] = ucur_ref[0]
    win_ref[HALO + TM:, :] = jnp.zeros((CONV_SLACK, D_MODEL), jnp.float32)
    off = HALO - (CONV_W - 1)
    rows = TM + 16
    for c in range(D_MODEL // LANES):
        cs = slice(c * LANES, (c + 1) * LANES)
        acc = jnp.zeros((TM, LANES), jnp.float32)
        for r in range(8):
            part = None
            for j in range(r, CONV_W, 8):
                term = win_ref[j - r:j - r + rows, cs] * wdw_ref[j:j + 1, cs]
                part = term if part is None else part + term
            slot = (c * 8 + r) % 2
            part_ref[slot] = part
            acc = acc + part_ref[slot, off + r:off + r + TM, :]
        conv_ref[:, cs] = acc
    y = conv_ref[...] + bdw_ref[...]
    y = _layernorm(y, lng_ref[...], lnb_ref[...])
    y = y * _sigmoid(y)
    m = jnp.dot(y.astype(jnp.bfloat16), w2_ref[...], preferred_element_type=jnp.float32) + b2_ref[...]
    h0 = _stream_tile(i, x_ref, meta_ref)
    h_ref[0] = _layernorm(ALPHA * h0 + m, pg_ref[...], pb_ref[...])


def _conv_layer(x, meta, w1, b1, wdw, bdw, lng, lnb, w2, b2, pg, pb):
    B, S, D = x.shape
    nt = S // TM + 1
    lp = nt * TM
    xmap = lambda b, i: (b, jnp.maximum(i - 1, 0), 0)
    full2 = lambda b, i: (0, 0)
    u = pl.pallas_call(
        _conv_in_kernel,
        grid=(B, nt),
        in_specs=[
            pl.BlockSpec((1, TM, D), xmap),
            pl.BlockSpec((N_META, D), full2),
            pl.BlockSpec((D, 2 * D), full2),
            pl.BlockSpec((1, 2 * D), full2),
        ],
        out_specs=pl.BlockSpec((1, TM, D), lambda b, i: (b, i, 0)),
        out_shape=jax.ShapeDtypeStruct((B, lp, D), jnp.float32),
        compiler_params=_cparams(("parallel", "parallel")),
        name="conv_in",
    )(x, meta, w1.astype(jnp.bfloat16), b1.reshape(1, -1))
    wdw_p = jnp.concatenate([wdw, jnp.zeros((HALO - CONV_W, D), wdw.dtype)], axis=0)
    vec = lambda a: a.reshape(1, D)
    h1 = pl.pallas_call(
        _conv_mid_kernel,
        grid=(B, nt),
        in_specs=[
            pl.BlockSpec((1, TM, D), lambda b, i: (b, i, 0)),
            pl.BlockSpec((1, HALO, D), lambda b, i: (b, jnp.maximum(i * (TM // HALO) - 1, 0), 0)),
            pl.BlockSpec((1, TM, D), xmap),
            pl.BlockSpec((N_META, D), full2),
            pl.BlockSpec((HALO, D), full2),
            pl.BlockSpec((1, D), full2),
            pl.BlockSpec((1, D), full2),
            pl.BlockSpec((1, D), full2),
            pl.BlockSpec((D, D), full2),
            pl.BlockSpec((1, D), full2),
            pl.BlockSpec((1, D), full2),
            pl.BlockSpec((1, D), full2),
        ],
        out_specs=pl.BlockSpec((1, TM, D), lambda b, i: (b, i, 0)),
        out_shape=jax.ShapeDtypeStruct((B, lp, D), jnp.float32),
        scratch_shapes=[
            pltpu.VMEM((HALO + TM + CONV_SLACK, D), jnp.float32),
            pltpu.VMEM((2, TM + 16, LANES), jnp.float32),
            pltpu.VMEM((TM, D), jnp.float32),
        ],
        compiler_params=_cparams(("parallel", "parallel")),
        name="conv_mid",
    )(u, u, x, meta, wdw_p, vec(bdw), vec(lng), vec(lnb), w2.astype(jnp.bfloat16), vec(b2), vec(pg), vec(pb))
    return h1


def _route_kernel(h_ref, whi_ref, wlo_ref, b_ref, info_ref, cnt_ref, htile_ref, carry_ref):
    step = pl.program_id(0)

    @pl.when(step == 0)
    def _():
        carry_ref[...] = jnp.zeros_like(carry_ref)

    h = h_ref[...]
    _to_token_tiles(htile_ref, h)
    hhi = h.astype(jnp.bfloat16)
    hlo = (h - hhi.astype(jnp.float32)).astype(jnp.bfloat16)
    lg = (jnp.dot(hhi, whi_ref[...], preferred_element_type=jnp.float32)
          + jnp.dot(hlo, whi_ref[...], preferred_element_type=jnp.float32)
          + jnp.dot(hhi, wlo_ref[...], preferred_element_type=jnp.float32)) + b_ref[...]
    lane_i = lax.broadcasted_iota(jnp.int32, (TM, LANES), 1)
    lane = lane_i.astype(jnp.float32)

    def first_argmax(v, vmax):
        return jnp.min(jnp.where(v == vmax, lane, float(LANES)), axis=1, keepdims=True)

    gmask = lane_i < N_GROUPS
    glog = jnp.where(gmask, lg, NEG)
    gmax = jnp.max(glog, axis=1, keepdims=True)
    gidx = first_argmax(glog, gmax)
    gsum = jnp.sum(jnp.where(gmask, jnp.exp(glog - gmax), 0.0), axis=1, keepdims=True)
    g_w = 1.0 / gsum
    elane = lane_i - N_GROUPS
    egrp = (elane >> 2).astype(jnp.float32)
    emask = (elane >= 0) & (elane < N_EXPERTS) & (egrp == gidx)
    elog = jnp.where(emask, lg, NEG)
    e1 = jnp.max(elog, axis=1, keepdims=True)
    i1 = first_argmax(elog, e1)
    elog2 = jnp.where(lane == i1, NEG, elog)
    e2 = jnp.max(elog2, axis=1, keepdims=True)
    i2 = first_argmax(elog2, e2)
    d = jnp.exp(e2 - e1)
    w1 = g_w / (1.0 + d)
    w2 = g_w * d / (1.0 + d)
    x1 = i1 - float(N_GROUPS)
    x2 = i2 - float(N_GROUPS)
    oh1 = jnp.where(lane == x1, 1.0, 0.0)
    oh2 = jnp.where(lane == x2, 1.0, 0.0)
    r_i = lax.broadcasted_iota(jnp.int32, (TM, TM), 0)
    c_i = lax.broadcasted_iota(jnp.int32, (TM, TM), 1)
    ltri = jnp.where(c_i < r_i, 1.0, 0.0).astype(jnp.bfloat16)
    pre1 = jnp.dot(ltri, oh1.astype(jnp.bfloat16), preferred_element_type=jnp.float32)
    pre2 = jnp.dot(ltri, oh2.astype(jnp.bfloat16), preferred_element_type=jnp.float32)
    cnt1 = jnp.sum(oh1, axis=0, keepdims=True)
    cnt2 = jnp.sum(oh2, axis=0, keepdims=True)
    carry = carry_ref[0:1, :]
    rank1 = jnp.sum(oh1 * (carry + pre1), axis=1, keepdims=True)
    rank2 = jnp.sum(oh2 * (carry + cnt1 + pre2), axis=1, keepdims=True)
    new_carry = carry + cnt1 + cnt2
    carry_ref[...] = jnp.broadcast_to(new_carry, carry_ref.shape)
    cnt_ref[...] = jnp.broadcast_to(new_carry, cnt_ref.shape)
    info = jnp.where(lane_i == 0, x1, 0.0)
    info = jnp.where(lane_i == 1, x2, info)
    info = jnp.where(lane_i == 2, w1, info)
    info = jnp.where(lane_i == 3, w2, info)
    info = jnp.where(lane_i == 4, rank1, info)
    info = jnp.where(lane_i == 5, rank2, info)
    info_ref[...] = info


def _route(h2d, w_group, b_group, w_expert, b_expert):
    T, D = h2d.shape
    wr = jnp.zeros((D, LANES), jnp.float32)
    wr = wr.at[:, :N_GROUPS].set(w_group).at[:, N_GROUPS:N_GROUPS + N_EXPERTS].set(w_expert)
    br = jnp.zeros((1, LANES), jnp.float32)
    br = br.at[0, :N_GROUPS].set(b_group).at[0, N_GROUPS:N_GROUPS + N_EXPERTS].set(b_expert)
    whi = wr.astype(jnp.bfloat16)
    wlo = (wr - whi.astype(jnp.float32)).astype(jnp.bfloat16)
    return pl.pallas_call(
        _route_kernel,
        grid=(T // TM,),
        in_specs=[
            pl.BlockSpec((TM, D), lambda i: (i, 0)),
            pl.BlockSpec((D, LANES), lambda i: (0, 0)),
            pl.BlockSpec((D, LANES), lambda i: (0, 0)),
            pl.BlockSpec((1, LANES), lambda i: (0, 0)),
        ],
        out_specs=[
            pl.BlockSpec((TM, LANES), lambda i: (i, 0)),
            pl.BlockSpec((8, LANES), lambda i: (0, 0)),
            pl.BlockSpec((TM * ROW_TILE, LANES), lambda i: (i, 0)),
        ],
        out_shape=[
            jax.ShapeDtypeStruct((T, LANES), jnp.float32),
            jax.ShapeDtypeStruct((8, LANES), jnp.float32),
            jax.ShapeDtypeStruct((T * ROW_TILE, LANES), jnp.float32),
        ],
        scratch_shapes=[pltpu.VMEM((8, LANES), jnp.float32)],
        compiler_params=_cparams(("arbitrary",)),
        name="route",
    )(h2d, whi, wlo, br)


def _dispatch_plan(info, cnt, n_rows):
    ex = info[:, 0:2].astype(jnp.int32)
    rank = info[:, 4:6].astype(jnp.int32)
    counts = cnt[0, :N_EXPERTS].astype(jnp.int32)
    padded = ((counts + TG - 1) // TG) * TG
    ends = jnp.cumsum(padded)
    starts = ends - padded
    pos = starts[ex] + rank
    i = jnp.arange(TG, dtype=jnp.int32)[None, :]
    pad_rows = jnp.where(i < (padded - counts)[:, None], (starts + counts)[:, None] + i, -1)
    tile_start = jnp.arange(n_rows // TG, dtype=jnp.int32) * TG
    tile_expert = jnp.minimum(
        jnp.sum((tile_start[:, None] >= ends[None, :]).astype(jnp.int32), axis=1), N_EXPERTS - 1)
    n_used = (ends[-1] // TG).astype(jnp.int32).reshape(1)
    return pos, pad_rows, tile_expert.astype(jnp.int32), n_used


def _to_token_tiles(ref, x):
    rows = x.shape[0]
    for j in range(ROW_TILE):
        ref[pl.ds(j, rows, stride=ROW_TILE), :] = x[:, j * LANES:(j + 1) * LANES]


def _from_token_tiles(ref, rows):
    return jnp.concatenate([ref[pl.ds(j, rows, stride=ROW_TILE), :] for j in range(ROW_TILE)], axis=1)


def _issue_row_copies(copy_pair, n_pairs):
    def start(i, c):
        a, b = copy_pair(i)
        a.start(priority=0)
        b.start(priority=1)
        return c

    def wait(i, c):
        a, b = copy_pair(i)
        a.wait()
        b.wait()
        return c

    lax.fori_loop(0, n_pairs, start, 0, unroll=4)
    lax.fori_loop(0, n_pairs, wait, 0, unroll=4)


def _dispatch_kernel(n_tiles, pads_per_step, tails_per_step, dst_ref, pad_ref, nu_ref, table_ref,
                     out_ref, zrow_ref, ztile_ref, sem, zsem):
    step = pl.program_id(0)

    @pl.when(step == 0)
    def _():
        zrow_ref[...] = jnp.zeros_like(zrow_ref)
        ztile_ref[...] = jnp.zeros_like(ztile_ref)

    def row_copy(i, k):
        return pltpu.make_async_copy(table_ref.at[i], out_ref.at[dst_ref[0, 0, 2 * i + k]], sem)

    _issue_row_copies(lambda i: (row_copy(i, 0), row_copy(i, 1)), MOVE_ROWS // 2)

    def tail_copy(k):
        tail = nu_ref[0] + step * tails_per_step + k
        copy = pltpu.make_async_copy(ztile_ref, out_ref.at[pl.ds(jnp.minimum(tail, n_tiles - 1) * TG, TG)], zsem)
        return tail < n_tiles, copy

    def pad_copy(r):
        return pltpu.make_async_copy(zrow_ref, out_ref.at[jnp.maximum(pad_ref[0, 0, r], 0)], zsem)

    def start(r, c):
        @pl.when(pad_ref[0, 0, r] >= 0)
        def _():
            pad_copy(r).start()
        return c

    def wait(r, c):
        @pl.when(pad_ref[0, 0, r] >= 0)
        def _():
            pad_copy(r).wait()
        return c

    for k in range(tails_per_step):
        live, copy = tail_copy(k)
        pl.when(live)(copy.start)
    lax.fori_loop(0, pads_per_step, start, 0)
    lax.fori_loop(0, pads_per_step, wait, 0)
    for k in range(tails_per_step):
        live, copy = tail_copy(k)
        pl.when(live)(copy.wait)


def _dispatch_rows(table, dst, pad_rows, n_used, n_out):
    n = dst.size
    steps = n // MOVE_ROWS
    pads_per_step = pl.cdiv(pad_rows.size, steps)
    tails_per_step = pl.cdiv(n_out // TG - n // TG, steps)
    pads = jnp.concatenate([pad_rows.reshape(-1),
                            jnp.full((steps * pads_per_step - pad_rows.size,), -1, jnp.int32)])
    return pl.pallas_call(
        functools.partial(_dispatch_kernel, n_out // TG, pads_per_step, tails_per_step),
        grid=(steps,),
        in_specs=[
            pl.BlockSpec((1, 1, MOVE_ROWS), lambda i: (i, 0, 0), memory_space=pltpu.SMEM),
            pl.BlockSpec((1, 1, pads_per_step), lambda i: (i, 0, 0), memory_space=pltpu.SMEM),
            pl.BlockSpec(memory_space=pltpu.SMEM),
            pl.BlockSpec((MOVE_ROWS // 2,) + table.shape[1:], lambda i: (i, 0, 0)),
        ],
        out_specs=pl.BlockSpec(memory_space=pl.ANY),
        out_shape=jax.ShapeDtypeStruct((n_out,) + table.shape[1:], table.dtype),
        scratch_shapes=[
            pltpu.VMEM(table.shape[1:], table.dtype),
            pltpu.VMEM((TG,) + table.shape[1:], table.dtype),
            pltpu.SemaphoreType.DMA(()),
            pltpu.SemaphoreType.DMA(()),
        ],
        compiler_params=_cparams(("arbitrary",)),
        name="dispatch_rows",
    )(dst.reshape(steps, 1, MOVE_ROWS), pads.reshape(steps, 1, pads_per_step), n_used, table)


def _gmm_kernel(te_ref, nu_ref, x_ref, w1_ref, w3_ref, w2_ref, y_ref, w1b_ref, w3b_ref, w2b_ref):
    j = pl.program_id(0)

    @pl.when((j == 0) | (te_ref[j] != te_ref[jnp.maximum(j - 1, 0)]))
    def _():
        w1b_ref[...] = w1_ref[0, 0].astype(jnp.bfloat16)
        w3b_ref[...] = w3_ref[0, 0].astype(jnp.bfloat16)
        w2b_ref[...] = w2_ref[0, 0].astype(jnp.bfloat16)

    @pl.when(j < nu_ref[0])
    def _():
        x = _from_token_tiles(x_ref, TG).astype(jnp.bfloat16)
        a = jnp.dot(x, w1b_ref[...], preferred_element_type=jnp.float32)
        g = jnp.dot(x, w3b_ref[...], preferred_element_type=jnp.float32)
        hh = (a * _sigmoid(a) * g).astype(jnp.bfloat16)
        _to_token_tiles(y_ref, jnp.dot(hh, w2b_ref[...], preferred_element_type=jnp.float32))

    @pl.when(j >= nu_ref[0])
    def _():
        y_ref[...] = jnp.zeros_like(y_ref)


def _gmm(xs2d, n_rows, layer, w1, w3, w2, tile_expert, n_used):
    D, de = w1.shape[2], w1.shape[3]
    blk = pl.BlockSpec((TG * ROW_TILE, LANES), lambda j, te, nu: (j, 0))
    grid_spec = pltpu.PrefetchScalarGridSpec(
        num_scalar_prefetch=2,
        grid=(n_rows // TG,),
        in_specs=[
            blk,
            pl.BlockSpec((1, 1, D, de), lambda j, te, nu: (layer, te[j], 0, 0)),
            pl.BlockSpec((1, 1, D, de), lambda j, te, nu: (layer, te[j], 0, 0)),
            pl.BlockSpec((1, 1, de, D), lambda j, te, nu: (layer, te[j], 0, 0)),
        ],
        out_specs=blk,
        scratch_shapes=[pltpu.VMEM((D, de), jnp.bfloat16), pltpu.VMEM((D, de), jnp.bfloat16),
                        pltpu.VMEM((de, D), jnp.bfloat16)],
    )
    return pl.pallas_call(
        _gmm_kernel,
        grid_spec=grid_spec,
        out_shape=jax.ShapeDtypeStruct((n_rows * ROW_TILE, LANES), jnp.float32),
        compiler_params=_cparams(("arbitrary",)),
        name="expert_mlp",
    )(tile_expert, n_used, xs2d, w1, w3, w2)


def _combine_kernel(n_steps, idx_ref, idx_next_ref, h_ref, info_ref, g_ref, b_ref, ys_ref, o_ref, ybuf_ref, sem):
    step = pl.program_id(0)

    def gather(idx, slot):
        def row_copy(r):
            return pltpu.make_async_copy(ys_ref.at[idx[0, 0, r]],
                                         ybuf_ref.at[slot, pl.ds(r * ROW_TILE, ROW_TILE), :], sem.at[slot])
        return lambda i: (row_copy(2 * i), row_copy(2 * i + 1))

    def start(copy_pair):
        def body(i, c):
            first, second = copy_pair(i)
            first.start(priority=0)
            second.start(priority=1)
            return c
        lax.fori_loop(0, TM, body, 0, unroll=4)

    def wait(copy_pair):
        def body(i, c):
            first, second = copy_pair(i)
            first.wait()
            second.wait()
            return c
        lax.fori_loop(0, TM, body, 0, unroll=4)

    @pl.when(step == 0)
    def _():
        start(gather(idx_ref, 0))

    for slot in range(2):
        @pl.when((step & 1) == slot)
        def _():
            @pl.when(step + 1 < n_steps)
            def _():
                start(gather(idx_next_ref, 1 - slot))

            wait(gather(idx_ref, slot))
            rows = ybuf_ref.at[slot]
            y0 = jnp.concatenate([rows[pl.ds(j, TM, stride=ROW_TILE), :] for j in range(ROW_TILE)], axis=1)
            y1 = jnp.concatenate([rows[pl.ds(TM * ROW_TILE + j, TM, stride=ROW_TILE), :]
                                  for j in range(ROW_TILE)], axis=1)
            info = info_ref[...]
            f = info[:, 2:3] * y0 + info[:, 3:4] * y1
            o_ref[...] = _layernorm(ALPHA * h_ref[...] + f, g_ref[...], b_ref[...])


def _combine_ln(h2d, ys, pos, info, g, b, batch, drop_front):
    T, D = h2d.shape
    nt = T // TM
    ntb = nt // batch
    if drop_front:
        n_steps = batch * (ntb - 1)
        tile = lambda s: s + s // (ntb - 1) + 1
    else:
        n_steps = nt
        tile = lambda s: s
    idx = jnp.concatenate([pos[:, 0].reshape(nt, TM), pos[:, 1].reshape(nt, TM)], axis=1).reshape(nt, 1, 2 * TM)
    idx_spec = lambda f: pl.BlockSpec((1, 1, 2 * TM), lambda s: (f(s), 0, 0), memory_space=pltpu.SMEM)
    return pl.pallas_call(
        functools.partial(_combine_kernel, n_steps),
        grid=(n_steps,),
        in_specs=[
            idx_spec(tile),
            idx_spec(lambda s: tile(jnp.minimum(s + 1, n_steps - 1))),
            pl.BlockSpec((TM, D), lambda s: (tile(s), 0)),
            pl.BlockSpec((TM, LANES), lambda s: (tile(s), 0)),
            pl.BlockSpec((1, D), lambda s: (0, 0)),
            pl.BlockSpec((1, D), lambda s: (0, 0)),
            pl.BlockSpec(memory_space=pl.ANY),
        ],
        out_specs=pl.BlockSpec((TM, D), lambda s: (s, 0)),
        out_shape=jax.ShapeDtypeStruct((n_steps * TM, D), jnp.float32),
        scratch_shapes=[pltpu.VMEM((2, 2 * TM * ROW_TILE, LANES), jnp.float32), pltpu.SemaphoreType.DMA((2,))],
        compiler_params=_cparams(("arbitrary",)),
        name="moe_combine_ln",
    )(idx, idx, h2d, info, g.reshape(1, D), b.reshape(1, D), ys)


def _moe_layer(h2d, layer, wg, bg, we, be, w1, w3, w2, pg, pb, batch, drop_front):
    T, D = h2d.shape
    n_rows = 2 * T + N_EXPERTS * TG
    info, cnt, h_tiles = _route(h2d, wg, bg, we, be)
    pos, pad_rows, tile_expert, n_used = _dispatch_plan(info, cnt, n_rows)
    xs = _dispatch_rows(h_tiles.reshape(T, ROW_TILE, LANES), pos, pad_rows, n_used, n_rows)
    ys = _gmm(xs.reshape(-1, LANES), n_rows, layer, w1, w3, w2, tile_expert, n_used)
    return _combine_ln(h2d, ys.reshape(n_rows, ROW_TILE, LANES), pos, info, pg, pb, batch, drop_front)


def _qkv_kernel(h_ref, wq_ref, wk_ref, wv_ref, cosT_ref, sinT_ref, cn_ref, s1_ref, s2_ref,
                q_ref, k_ref, v_ref):
    hb = h_ref[0].astype(jnp.bfloat16)
    half = ROT_DIM // 2
    nt_dims = (((1,), (1,)), ((), ()))
    qT = lax.dot_general(wq_ref[...], hb, nt_dims, preferred_element_type=jnp.float32)
    cosT = cosT_ref[...]
    sinT = sinT_ref[...]
    scale = HEAD_DIM ** -0.5 * math.log2(math.e)
    for hh in range(N_HEADS):
        parts = []
        for c in range(2):
            r0 = hh * 2 * HEAD_DIM + c * HEAD_DIM
            t1 = qT[r0:r0 + half]
            t2 = qT[r0 + half:r0 + 2 * half]
            parts += [t1 * cosT - t2 * sinT, t1 * sinT + t2 * cosT, qT[r0 + 2 * half:r0 + HEAD_DIM]]
        q_ref[0, hh, 0] = (jnp.concatenate(parts, axis=0) * scale).astype(jnp.bfloat16)
    k = jnp.dot(hb, wk_ref[...], preferred_element_type=jnp.float32)
    cn = cn_ref[...]
    s1 = s1_ref[...]
    s2 = s2_ref[...]
    for hh in range(N_HEADS):
        kb = k[:, hh * LANES:(hh + 1) * LANES]
        kr = kb * cn + pltpu.roll(kb, LANES - half, 1) * s1 + pltpu.roll(kb, half, 1) * s2
        k_ref[0, hh, 0] = kr.astype(jnp.bfloat16)
    vT = lax.dot_general(wv_ref[...], hb, nt_dims, preferred_element_type=jnp.float32)
    extra = lax.broadcasted_iota(jnp.int32, (V_ROWS - V_DIM, TM), 0)
    ones_row = jnp.where(extra == 0, 1.0, 0.0).astype(jnp.bfloat16)
    for hh in range(N_HEADS):
        v_ref[0, hh, 0, 0:V_DIM, :] = vT[hh * V_DIM:(hh + 1) * V_DIM].astype(jnp.bfloat16)
        v_ref[0, hh, 0, V_DIM:V_ROWS, :] = ones_row


def _rope_tables(lp):
    half = ROT_DIM // 2
    pos = jnp.arange(lp, dtype=jnp.float32) - PAD0
    inv = ROPE_THETA ** (-jnp.arange(0, ROT_DIM, 2, dtype=jnp.float32) / ROT_DIM)
    ang = pos[:, None] * inv[None, :]
    cos, sin = jnp.cos(ang), jnp.sin(ang)
    lane = jnp.arange(LANES) % HEAD_DIM
    is1 = lane < half
    is2 = (lane >= half) & (lane < ROT_DIM)
    fi = jnp.where(is1, lane, jnp.where(is2, lane - half, 0))
    cn = jnp.where(is1 | is2, cos[:, fi], 1.0)
    s1 = jnp.where(is1, -sin[:, fi], 0.0)
    s2 = jnp.where(is2, sin[:, fi], 0.0)
    return cos.T, sin.T, cn, s1, s2


def _qkv(h, wq, wk, wv):
    B, lp, D = h.shape
    nt = lp // TM
    cosT, sinT, cn, s1, s2 = _rope_tables(lp)
    full2 = lambda b, i: (0, 0)
    hd2 = 2 * HEAD_DIM
    out5 = lambda r, c: jax.ShapeDtypeStruct((B, N_HEADS, nt, r, c), jnp.bfloat16)
    spec5 = lambda r, c: pl.BlockSpec((1, N_HEADS, 1, r, c), lambda b, i: (b, 0, i, 0, 0))
    return pl.pallas_call(
        _qkv_kernel,
        grid=(B, nt),
        in_specs=[
            pl.BlockSpec((1, TM, D), lambda b, i: (b, i, 0)),
            pl.BlockSpec((D, D), full2),
            pl.BlockSpec((D, D), full2),
            pl.BlockSpec((D, D), full2),
            pl.BlockSpec((ROT_DIM // 2, TM), lambda b, i: (0, i)),
            pl.BlockSpec((ROT_DIM // 2, TM), lambda b, i: (0, i)),
            pl.BlockSpec((TM, LANES), lambda b, i: (i, 0)),
            pl.BlockSpec((TM, LANES), lambda b, i: (i, 0)),
            pl.BlockSpec((TM, LANES), lambda b, i: (i, 0)),
        ],
        out_specs=[spec5(hd2, TM), spec5(TM, hd2), spec5(V_ROWS, TM)],
        out_shape=[out5(hd2, TM), out5(TM, hd2), out5(V_ROWS, TM)],
        compiler_params=_cparams(("parallel", "parallel")),
        name="qkv_proj",
    )(h, wq.T.astype(jnp.bfloat16), wk.astype(jnp.bfloat16), wv.T.astype(jnp.bfloat16), cosT, sinT, cn, s1, s2)


MASK_PAD, MASK_DIAG, MASK_ALL = 1, 2, 4
ATT_UNROLLS = (16, 4, 2)
SOFTMAX_ROWS = 32


MASK_VALUE = -1e30
V_ROWS = V_DIM + 16
N_CHUNK_T = TM // CHUNK


def _attn_mask_tables():
    krow = jnp.arange(TM)[:, None]
    col = jnp.arange(LANES)[None, :]
    diag = (col < N_CHUNK_T) & (krow // CHUNK == col)
    pad = (col == N_CHUNK_T) & (krow < PAD0)
    everything = jnp.broadcast_to(col == N_CHUNK_T + 1, (TM, LANES))
    none = jnp.zeros((TM, LANES), bool)
    sel = jnp.stack([none, pad, diag, pad | diag, everything]).astype(jnp.bfloat16)
    row = jnp.arange(LANES)[:, None]
    qchunk = (jnp.arange(2 * TM)[None, :] % TM) // CHUNK
    hidden = ((row < N_CHUNK_T) & (row > qchunk)) | (row == N_CHUNK_T) | (row == N_CHUNK_T + 1)
    val = jnp.where(hidden, MASK_VALUE, 0.0).astype(jnp.bfloat16)
    return sel, val


def _attn_kernel(lam_init, q_ref, k_ref, v_ref, sel_ref, val_ref, lam_ref, g_ref, o_ref,
                 qcat_ref, s_ref, mt_ref, p_ref, alpha_ref, m_ref, acc_ref):
    qi = pl.program_id(2)
    W = 2 * TM
    F = 2 * HEAD_DIM
    q = q_ref[0, 0, 0]
    zero = jnp.zeros((HEAD_DIM, TM), jnp.bfloat16)
    qcat_ref[0:F, 0:TM] = jnp.concatenate([q[:HEAD_DIM], zero], axis=0)
    qcat_ref[0:F, TM:W] = jnp.concatenate([zero, q[HEAD_DIM:]], axis=0)
    qcat_ref[F:2 * F, :] = val_ref[...]
    s_ref[1] = jnp.full((TM, W), -jnp.inf, jnp.float32)
    p_ref[0] = jnp.zeros((TM, W), jnp.bfloat16)
    alpha_ref[0] = jnp.ones((8, W), jnp.float32)
    mt_ref[1] = jnp.full((8, W), -jnp.inf, jnp.float32)
    m_ref[...] = jnp.full(m_ref.shape, jnp.finfo(jnp.float32).min, jnp.float32)
    acc_ref[...] = jnp.zeros_like(acc_ref)

    def stages(t, cur):
        prv = 1 - cur
        kind = jnp.where(t > qi, MASK_ALL, jnp.where(t == 0, MASK_PAD, 0) + jnp.where(t == qi, MASK_DIAG, 0))
        lhs = jnp.concatenate([k_ref[0, 0, jnp.minimum(t, qi)], sel_ref[kind]], axis=1)
        s_new = jnp.dot(lhs, qcat_ref[...], preferred_element_type=jnp.float32)
        s_ref[cur] = s_new
        mt_ref[cur, 0:1, :] = jnp.max(s_new, axis=0, keepdims=True)
        vt = v_ref[0, 0, jnp.clip(t - 2, 0, qi)]
        pv = jnp.dot(vt, p_ref[cur], preferred_element_type=jnp.float32)
        acc_ref[...] = alpha_ref[cur, 0:1, :] * acc_ref[...] + pv
        m_old = m_ref[0:1, :]
        m_new = jnp.maximum(m_old, mt_ref[prv, 0:1, :])
        for r in range(0, TM, SOFTMAX_ROWS):
            p = jnp.exp2(s_ref[prv, r:r + SOFTMAX_ROWS, :] - m_new)
            p_ref[prv, r:r + SOFTMAX_ROWS, :] = p.astype(jnp.bfloat16)
        m_ref[0:1, :] = m_new
        alpha_ref[prv, 0:1, :] = jnp.exp2(m_old - m_new)

    def unrolled(first, unroll):
        def body(i, carry):
            for u in range(unroll):
                stages(first + i * unroll + u, u & 1)
            return carry
        return body

    n_iter = qi + 3
    done = 0
    for unroll in ATT_UNROLLS[:-1]:
        n = (n_iter - done) // unroll
        lax.fori_loop(0, n, unrolled(done, unroll), 0)
        done = done + n * unroll
    last = ATT_UNROLLS[-1]
    lax.fori_loop(0, (n_iter - done + last - 1) // last, unrolled(done, last), 0)

    lp = lam_ref[...]
    lam = (jnp.exp(jnp.sum(lp[0:1] * lp[1:2], axis=1, keepdims=True))
           - jnp.exp(jnp.sum(lp[2:3] * lp[3:4], axis=1, keepdims=True)) + lam_init)
    l = acc_ref[V_DIM:V_DIM + 1, :]
    l = jnp.where(l == 0.0, 1.0, l)
    acc = acc_ref[0:V_DIM, :]
    o = acc[:, 0:TM] / l[:, 0:TM] - lam * (acc[:, TM:W] / l[:, TM:W])
    ms = jnp.mean(o * o, axis=0, keepdims=True)
    y = o * lax.rsqrt(ms + EPS) * g_ref[...] * (1.0 - lam_init)
    o_ref[0] = y.astype(o_ref.dtype)


def _attention(q5, k5, v5, lam_params, subln_g, lam_init):
    B, H, nt, _, _ = q5.shape
    lp = nt * TM
    kernel = functools.partial(_attn_kernel, lam_init)
    sel, val = _attn_mask_tables()
    return pl.pallas_call(
        kernel,
        grid=(B, H, nt),
        in_specs=[
            pl.BlockSpec((1, 1, 1, 2 * HEAD_DIM, TM), lambda b, h, i: (b, h, i, 0, 0)),
            pl.BlockSpec((1, 1, nt, TM, 2 * HEAD_DIM), lambda b, h, i: (b, h, 0, 0, 0)),
            pl.BlockSpec((1, 1, nt, V_ROWS, TM), lambda b, h, i: (b, h, 0, 0, 0)),
            pl.BlockSpec((5, TM, LANES), lambda b, h, i: (0, 0, 0)),
            pl.BlockSpec((LANES, 2 * TM), lambda b, h, i: (0, 0)),
            pl.BlockSpec((4, HEAD_DIM), lambda b, h, i: (0, 0)),
            pl.BlockSpec((V_DIM, 1), lambda b, h, i: (0, 0)),
        ],
        out_specs=pl.BlockSpec((1, V_DIM, TM), lambda b, h, i: (b, h, i)),
        out_shape=jax.ShapeDtypeStruct((B, H * V_DIM, lp), jnp.bfloat16),
        scratch_shapes=[
            pltpu.VMEM((4 * HEAD_DIM, 2 * TM), jnp.bfloat16),
            pltpu.VMEM((2, TM, 2 * TM), jnp.float32),
            pltpu.VMEM((2, 8, 2 * TM), jnp.float32),
            pltpu.VMEM((2, TM, 2 * TM), jnp.bfloat16),
            pltpu.VMEM((2, 8, 2 * TM), jnp.float32),
            pltpu.VMEM((8, 2 * TM), jnp.float32),
            pltpu.VMEM((V_ROWS, 2 * TM), jnp.float32),
        ],
        compiler_params=_cparams(("parallel", "parallel", "arbitrary")),
        name="diff_attention",
    )(q5, k5, v5, sel, val, lam_params, subln_g.reshape(V_DIM, 1))


def _attn_out_kernel(o_ref, w_ref, h_ref, g_ref, b_ref, out_ref):
    m = lax.dot_general(o_ref[0], w_ref[...], (((0,), (0,)), ((), ())), preferred_element_type=jnp.float32)
    out_ref[0] = _layernorm(ALPHA * h_ref[0] + m, g_ref[...], b_ref[...])


def _attn_out(o, w_o, h, g, b):
    B, lp, D = h.shape
    nt = lp // TM
    full2 = lambda bb, i: (0, 0)
    tile = pl.BlockSpec((1, TM, D), lambda bb, i: (bb, i, 0))
    return pl.pallas_call(
        _attn_out_kernel,
        grid=(B, nt),
        in_specs=[pl.BlockSpec((1, D, TM), lambda bb, i: (bb, 0, i)), pl.BlockSpec((D, D), full2), tile,
                  pl.BlockSpec((1, D), full2), pl.BlockSpec((1, D), full2)],
        out_specs=tile,
        out_shape=jax.ShapeDtypeStruct((B, lp, D), jnp.float32),
        compiler_params=_cparams(("parallel", "parallel")),
        name="attn_out_ln",
    )(o, w_o.astype(jnp.bfloat16), h, g.reshape(1, D), b.reshape(1, D))


def kernel(x, meta_tokens, conv_w_pw1, conv_b_pw1, conv_w_dw, conv_b_dw, conv_ln_g, conv_ln_b, conv_w_pw2, conv_b_pw2, kv_w_k, kv_w_v, attn_w_q, attn_lam_q1, attn_lam_k1, attn_lam_q2, attn_lam_k2, attn_subln_g, attn_w_o, post_ln_g, post_ln_b, moe_w_group, moe_b_group, moe_w_expert, moe_b_expert, moe_w1, moe_w3, moe_w2):
    B, S, D = x.shape
    assert D == D_MODEL and S % TM == 0 and S % CHUNK == 0
    assert conv_w_pw1.shape[0] == 1 and attn_w_q.shape[0] == 1 and post_ln_g.shape[0] == DEPTH
    lp = S + TM

    def moe(h, l, drop_front):
        out = _moe_layer(h.reshape(B * lp, D), l, moe_w_group[l], moe_b_group[l], moe_w_expert[l],
                         moe_b_expert[l], moe_w1, moe_w3, moe_w2,
                         post_ln_g[l, 1], post_ln_b[l, 1], B, drop_front)
        return out.reshape(B, -1, D)

    h = _conv_layer(x, meta_tokens, conv_w_pw1[0], conv_b_pw1[0], conv_w_dw[0], conv_b_dw[0],
                    conv_ln_g[0], conv_ln_b[0], conv_w_pw2[0], conv_b_pw2[0],
                    post_ln_g[0, 0], post_ln_b[0, 0])
    h = moe(h, 0, False)
    q5, k5, v5 = _qkv(h, attn_w_q[0], kv_w_k, kv_w_v)
    lam_init = 0.8 - 0.6 * math.exp(-0.3 * 1)
    lam_params = jnp.stack([attn_lam_q1[0], attn_lam_k1[0], attn_lam_q2[0], attn_lam_k2[0]])
    o = _attention(q5, k5, v5, lam_params, attn_subln_g[0], lam_init)
    h = _attn_out(o, attn_w_o[0], h, post_ln_g[1, 0], post_ln_b[1, 0])
    return moe(h, 1, True)
```

```python
import functools
import math

import jax
import jax.numpy as jnp
from jax import lax
from jax.experimental import pallas as pl
from jax.experimental.pallas import tpu as pltpu

D_MODEL = 1024
N_META = 16
CHUNK = 64
DEPTH = 2
ALPHA = (2.0 * DEPTH) ** 0.25
CONV_W = 31
N_HEADS = 8
HEAD_DIM = 64
V_DIM = 128
ROT_DIM = 16
ROPE_THETA = 500000.0
N_GROUPS = 4
EXPERTS_PER_GROUP = 4
N_EXPERTS = 16
D_EXPERT = 512
EPS = 1e-5

TM = 256
PAD0 = TM - N_META
HALO = 32
CONV_SLACK = 8
TG = 256
ROW_TILE = 8
MOVE_ROWS = 512
LANES = 128
NEG = -1e30
VMEM_LIMIT = 48 * 1024 * 1024


def _cparams(sem):
    return pltpu.CompilerParams(dimension_semantics=sem, vmem_limit_bytes=VMEM_LIMIT)


def _layernorm(x, g, b):
    mu = jnp.mean(x, axis=-1, keepdims=True)
    xc = x - mu
    var = jnp.mean(xc * xc, axis=-1, keepdims=True)
    return xc * lax.rsqrt(var + EPS) * g + b


def _sigmoid(x):
    return 1.0 / (1.0 + jnp.exp(-x))


def _front_tile(meta_ref):
    return jnp.concatenate([jnp.zeros((PAD0, D_MODEL), jnp.float32), meta_ref[...]], axis=0)


def _stream_tile(i, x_ref, meta_ref):
    return jnp.where(i == 0, _front_tile(meta_ref), x_ref[0])


def _conv_in_kernel(x_ref, meta_ref, w_ref, b_ref, u_ref):
    i = pl.program_id(1)
    xt = _stream_tile(i, x_ref, meta_ref)
    h = jnp.dot(xt.astype(jnp.bfloat16), w_ref[...], preferred_element_type=jnp.float32) + b_ref[...]
    u = h[:, :D_MODEL] * _sigmoid(h[:, D_MODEL:])
    row = lax.broadcasted_iota(jnp.int32, (TM, 1), 0)
    u_ref[0] = jnp.where((i == 0) & (row < PAD0), 0.0, u)


def _conv_mid_kernel(ucur_ref, uprev_ref, x_ref, meta_ref, wdw_ref, bdw_ref, lng_ref, lnb_ref,
                     w2_ref, b2_ref, pg_ref, pb_ref, h_ref, win_ref, part_ref, conv_ref):
    i = pl.program_id(1)
    win_ref[0:HALO, :] = jnp.where(i == 0, 0.0, uprev_ref[0])
    win_ref[HALO:HALO + TM, :] = ucur_ref[0]
    win_ref[HALO + TM:, :] = jnp.zeros((CONV_SLACK, D_MODEL), jnp.float32)
    off = HALO - (CONV_W - 1)
    rows = TM + 16
    for c in range(D_MODEL // LANES):
        cs = slice(c * LANES, (c + 1) * LANES)
        acc = jnp.zeros((TM, LANES), jnp.float32)
        for r in range(8):
            part = None
            for j in range(r, CONV_W, 8):
                term = win_ref[j - r:j - r + rows, cs] * wdw_ref[j:j + 1, cs]
                part = term if part is None else part + term
            slot = (c * 8 + r) % 2
            part_ref[slot] = part
            acc = acc + part_ref[slot, off + r:off + r + TM, :]
        conv_ref[:, cs] = acc
    y = conv_ref[...] + bdw_ref[...]
    y = _layernorm(y, lng_ref[...], lnb_ref[...])
    y = y * _sigmoid(y)
    m = jnp.dot(y.astype(jnp.bfloat16), w2_ref[...], preferred_element_type=jnp.float32) + b2_ref[...]
    h0 = _stream_tile(i, x_ref, meta_ref)
    h_ref[0] = _layernorm(ALPHA * h0 + m, pg_ref[...], pb_ref[...])


def _conv_layer(x, meta, w1, b1, wdw, bdw, lng, lnb, w2, b2, pg, pb):
    B, S, D = x.shape
    nt = S // TM + 1
    lp = nt * TM
    xmap = lambda b, i: (b, jnp.maximum(i - 1, 0), 0)
    full2 = lambda b, i: (0, 0)
    u = pl.pallas_call(
        _conv_in_kernel,
        grid=(B, nt),
        in_specs=[
            pl.BlockSpec((1, TM, D), xmap),
            pl.BlockSpec((N_META, D), full2),
            pl.BlockSpec((D, 2 * D), full2),
            pl.BlockSpec((1, 2 * D), full2),
        ],
        out_specs=pl.BlockSpec((1, TM, D), lambda b, i: (b, i, 0)),
        out_shape=jax.ShapeDtypeStruct((B, lp, D), jnp.float32),
        compiler_params=_cparams(("parallel", "parallel")),
        name="conv_in",
    )(x, meta, w1.astype(jnp.bfloat16), b1.reshape(1, -1))
    wdw_p = jnp.concatenate([wdw, jnp.zeros((HALO - CONV_W, D), wdw.dtype)], axis=0)
    vec = lambda a: a.reshape(1, D)
    h1 = pl.pallas_call(
        _conv_mid_kernel,
        grid=(B, nt),
        in_specs=[
            pl.BlockSpec((1, TM, D), lambda b, i: (b, i, 0)),
            pl.BlockSpec((1, HALO, D), lambda b, i: (b, jnp.maximum(i * (TM // HALO) - 1, 0), 0)),
            pl.BlockSpec((1, TM, D), xmap),
            pl.BlockSpec((N_META, D), full2),
            pl.BlockSpec((HALO, D), full2),
            pl.BlockSpec((1, D), full2),
            pl.BlockSpec((1, D), full2),
            pl.BlockSpec((1, D), full2),
            pl.BlockSpec((D, D), full2),
            pl.BlockSpec((1, D), full2),
            pl.BlockSpec((1, D), full2),
            pl.BlockSpec((1, D), full2),
        ],
        out_specs=pl.BlockSpec((1, TM, D), lambda b, i: (b, i, 0)),
        out_shape=jax.ShapeDtypeStruct((B, lp, D), jnp.float32),
        scratch_shapes=[
            pltpu.VMEM((HALO + TM + CONV_SLACK, D), jnp.float32),
            pltpu.VMEM((2, TM + 16, LANES), jnp.float32),
            pltpu.VMEM((TM, D), jnp.float32),
        ],
        compiler_params=_cparams(("parallel", "parallel")),
        name="conv_mid",
    )(u, u, x, meta, wdw_p, vec(bdw), vec(lng), vec(lnb), w2.astype(jnp.bfloat16), vec(b2), vec(pg), vec(pb))
    return h1


def _route_kernel(h_ref, whi_ref, wlo_ref, b_ref, info_ref, cnt_ref, carry_ref):
    step = pl.program_id(0)

    @pl.when(step == 0)
    def _():
        carry_ref[...] = jnp.zeros_like(carry_ref)

    h = h_ref[...]
    hhi = h.astype(jnp.bfloat16)
    hlo = (h - hhi.astype(jnp.float32)).astype(jnp.bfloat16)
    lg = (jnp.dot(hhi, whi_ref[...], preferred_element_type=jnp.float32)
          + jnp.dot(hlo, whi_ref[...], preferred_element_type=jnp.float32)
          + jnp.dot(hhi, wlo_ref[...], preferred_element_type=jnp.float32)) + b_ref[...]
    lane_i = lax.broadcasted_iota(jnp.int32, (TM, LANES), 1)
    lane = lane_i.astype(jnp.float32)

    def first_argmax(v, vmax):
        return jnp.min(jnp.where(v == vmax, lane, float(LANES)), axis=1, keepdims=True)

    gmask = lane_i < N_GROUPS
    glog = jnp.where(gmask, lg, NEG)
    gmax = jnp.max(glog, axis=1, keepdims=True)
    gidx = first_argmax(glog, gmax)
    gsum = jnp.sum(jnp.where(gmask, jnp.exp(glog - gmax), 0.0), axis=1, keepdims=True)
    g_w = 1.0 / gsum
    elane = lane_i - N_GROUPS
    egrp = (elane >> 2).astype(jnp.float32)
    emask = (elane >= 0) & (elane < N_EXPERTS) & (egrp == gidx)
    elog = jnp.where(emask, lg, NEG)
    e1 = jnp.max(elog, axis=1, keepdims=True)
    i1 = first_argmax(elog, e1)
    elog2 = jnp.where(lane == i1, NEG, elog)
    e2 = jnp.max(elog2, axis=1, keepdims=True)
    i2 = first_argmax(elog2, e2)
    d = jnp.exp(e2 - e1)
    w1 = g_w / (1.0 + d)
    w2 = g_w * d / (1.0 + d)
    x1 = i1 - float(N_GROUPS)
    x2 = i2 - float(N_GROUPS)
    oh1 = jnp.where(lane == x1, 1.0, 0.0)
    oh2 = jnp.where(lane == x2, 1.0, 0.0)
    r_i = lax.broadcasted_iota(jnp.int32, (TM, TM), 0)
    c_i = lax.broadcasted_iota(jnp.int32, (TM, TM), 1)
    ltri = jnp.where(c_i < r_i, 1.0, 0.0).astype(jnp.bfloat16)
    pre1 = jnp.dot(ltri, oh1.astype(jnp.bfloat16), preferred_element_type=jnp.float32)
    pre2 = jnp.dot(ltri, oh2.astype(jnp.bfloat16), preferred_element_type=jnp.float32)
    cnt1 = jnp.sum(oh1, axis=0, keepdims=True)
    cnt2 = jnp.sum(oh2, axis=0, keepdims=True)
    carry = carry_ref[0:1, :]
    rank1 = jnp.sum(oh1 * (carry + pre1), axis=1, keepdims=True)
    rank2 = jnp.sum(oh2 * (carry + cnt1 + pre2), axis=1, keepdims=True)
    new_carry = carry + cnt1 + cnt2
    carry_ref[...] = jnp.broadcast_to(new_carry, carry_ref.shape)
    cnt_ref[...] = jnp.broadcast_to(new_carry, cnt_ref.shape)
    info = jnp.where(lane_i == 0, x1, 0.0)
    info = jnp.where(lane_i == 1, x2, info)
    info = jnp.where(lane_i == 2, w1, info)
    info = jnp.where(lane_i == 3, w2, info)
    info = jnp.where(lane_i == 4, rank1, info)
    info = jnp.where(lane_i == 5, rank2, info)
    info_ref[...] = info


def _route(h2d, w_group, b_group, w_expert, b_expert):
    T, D = h2d.shape
    wr = jnp.zeros((D, LANES), jnp.float32)
    wr = wr.at[:, :N_GROUPS].set(w_group).at[:, N_GROUPS:N_GROUPS + N_EXPERTS].set(w_expert)
    br = jnp.zeros((1, LANES), jnp.float32)
    br = br.at[0, :N_GROUPS].set(b_group).at[0, N_GROUPS:N_GROUPS + N_EXPERTS].set(b_expert)
    whi = wr.astype(jnp.bfloat16)
    wlo = (wr - whi.astype(jnp.float32)).astype(jnp.bfloat16)
    return pl.pallas_call(
        _route_kernel,
        grid=(T // TM,),
        in_specs=[
            pl.BlockSpec((TM, D), lambda i: (i, 0)),
            pl.BlockSpec((D, LANES), lambda i: (0, 0)),
            pl.BlockSpec((D, LANES), lambda i: (0, 0)),
            pl.BlockSpec((1, LANES), lambda i: (0, 0)),
        ],
        out_specs=[
            pl.BlockSpec((TM, LANES), lambda i: (i, 0)),
            pl.BlockSpec((8, LANES), lambda i: (0, 0)),
        ],
        out_shape=[
            jax.ShapeDtypeStruct((T, LANES), jnp.float32),
            jax.ShapeDtypeStruct((8, LANES), jnp.float32),
        ],
        scratch_shapes=[pltpu.VMEM((8, LANES), jnp.float32)],
        compiler_params=_cparams(("arbitrary",)),
        name="route",
    )(h2d, whi, wlo, br)


def _dispatch_plan(info, cnt, n_rows):
    ex = info[:, 0:2].astype(jnp.int32)
    rank = info[:, 4:6].astype(jnp.int32)
    counts = cnt[0, :N_EXPERTS].astype(jnp.int32)
    padded = ((counts + TG - 1) // TG) * TG
    ends = jnp.cumsum(padded)
    starts = ends - padded
    pos = starts[ex] + rank
    i = jnp.arange(TG, dtype=jnp.int32)[None, :]
    pad_rows = jnp.where(i < (padded - counts)[:, None], (starts + counts)[:, None] + i, -1)
    tile_start = jnp.arange(n_rows // TG, dtype=jnp.int32) * TG
    tile_expert = jnp.minimum(
        jnp.sum((tile_start[:, None] >= ends[None, :]).astype(jnp.int32), axis=1), N_EXPERTS - 1)
    n_used = (ends[-1] // TG).astype(jnp.int32).reshape(1)
    return pos, pad_rows, tile_expert.astype(jnp.int32), n_used


def _to_token_tiles(ref, x):
    rows = x.shape[0]
    for j in range(ROW_TILE):
        ref[pl.ds(j, rows, stride=ROW_TILE), :] = x[:, j * LANES:(j + 1) * LANES]


def _from_token_tiles(ref, rows):
    return jnp.concatenate([ref[pl.ds(j, rows, stride=ROW_TILE), :] for j in range(ROW_TILE)], axis=1)


def _issue_row_copies(copy_pair, n_pairs):
    def start(i, c):
        a, b = copy_pair(i)
        a.start(priority=0)
        b.start(priority=1)
        return c

    def wait(i, c):
        a, b = copy_pair(i)
        a.wait()
        b.wait()
        return c

    lax.fori_loop(0, n_pairs, start, 0, unroll=4)
    lax.fori_loop(0, n_pairs, wait, 0, unroll=4)


def _dispatch_kernel(n_tiles, pads_per_step, tails_per_step, dst_ref, pad_ref, nu_ref, h_ref,
                     out_ref, tok_ref, zrow_ref, ztile_ref, sem, zsem):
    step = pl.program_id(0)
    _to_token_tiles(tok_ref, h_ref[...])

    @pl.when(step == 0)
    def _():
        zrow_ref[...] = jnp.zeros_like(zrow_ref)
        ztile_ref[...] = jnp.zeros_like(ztile_ref)

    def row_copy(i, k):
        return pltpu.make_async_copy(tok_ref.at[pl.ds(i * ROW_TILE, ROW_TILE), :],
                                     out_ref.at[dst_ref[0, 0, 2 * i + k]], sem)

    _issue_row_copies(lambda i: (row_copy(i, 0), row_copy(i, 1)), MOVE_ROWS // 2)

    def tail_copy(k):
        tail = nu_ref[0] + step * tails_per_step + k
        copy = pltpu.make_async_copy(ztile_ref, out_ref.at[pl.ds(jnp.minimum(tail, n_tiles - 1) * TG, TG)], zsem)
        return tail < n_tiles, copy

    def pad_copy(r):
        return pltpu.make_async_copy(zrow_ref, out_ref.at[jnp.maximum(pad_ref[0, 0, r], 0)], zsem)

    def start(r, c):
        @pl.when(pad_ref[0, 0, r] >= 0)
        def _():
            pad_copy(r).start()
        return c

    def wait(r, c):
        @pl.when(pad_ref[0, 0, r] >= 0)
        def _():
            pad_copy(r).wait()
        return c

    for k in range(tails_per_step):
        live, copy = tail_copy(k)
        pl.when(live)(copy.start)
    lax.fori_loop(0, pads_per_step, start, 0)
    lax.fori_loop(0, pads_per_step, wait, 0)
    for k in range(tails_per_step):
        live, copy = tail_copy(k)
        pl.when(live)(copy.wait)


def _dispatch_rows(h2d, dst, pad_rows, n_used, n_out):
    tile = (ROW_TILE, LANES)
    n = dst.size
    steps = n // MOVE_ROWS
    pads_per_step = pl.cdiv(pad_rows.size, steps)
    tails_per_step = pl.cdiv(n_out // TG - n // TG, steps)
    pads = jnp.concatenate([pad_rows.reshape(-1),
                            jnp.full((steps * pads_per_step - pad_rows.size,), -1, jnp.int32)])
    return pl.pallas_call(
        functools.partial(_dispatch_kernel, n_out // TG, pads_per_step, tails_per_step),
        grid=(steps,),
        in_specs=[
            pl.BlockSpec((1, 1, MOVE_ROWS), lambda i: (i, 0, 0), memory_space=pltpu.SMEM),
            pl.BlockSpec((1, 1, pads_per_step), lambda i: (i, 0, 0), memory_space=pltpu.SMEM),
            pl.BlockSpec(memory_space=pltpu.SMEM),
            pl.BlockSpec((MOVE_ROWS // 2, h2d.shape[1]), lambda i: (i, 0)),
        ],
        out_specs=pl.BlockSpec(memory_space=pl.ANY),
        out_shape=jax.ShapeDtypeStruct((n_out,) + tile, h2d.dtype),
        scratch_shapes=[
            pltpu.VMEM((MOVE_ROWS // 2 * ROW_TILE, LANES), h2d.dtype),
            pltpu.VMEM(tile, h2d.dtype),
            pltpu.VMEM((TG,) + tile, h2d.dtype),
            pltpu.SemaphoreType.DMA(()),
            pltpu.SemaphoreType.DMA(()),
        ],
        compiler_params=_cparams(("arbitrary",)),
        name="dispatch_rows",
    )(dst.reshape(steps, 1, MOVE_ROWS), pads.reshape(steps, 1, pads_per_step), n_used, h2d)


def _gmm_kernel(te_ref, nu_ref, x_ref, w1_ref, w3_ref, w2_ref, y_ref, w1b_ref, w3b_ref, w2b_ref):
    j = pl.program_id(0)

    @pl.when((j == 0) | (te_ref[j] != te_ref[jnp.maximum(j - 1, 0)]))
    def _():
        w1b_ref[...] = w1_ref[0, 0].astype(jnp.bfloat16)
        w3b_ref[...] = w3_ref[0, 0].astype(jnp.bfloat16)
        w2b_ref[...] = w2_ref[0, 0].astype(jnp.bfloat16)

    @pl.when(j < nu_ref[0])
    def _():
        x = _from_token_tiles(x_ref, TG).astype(jnp.bfloat16)
        a = jnp.dot(x, w1b_ref[...], preferred_element_type=jnp.float32)
        g = jnp.dot(x, w3b_ref[...], preferred_element_type=jnp.float32)
        hh = (a * _sigmoid(a) * g).astype(jnp.bfloat16)
        _to_token_tiles(y_ref, jnp.dot(hh, w2b_ref[...], preferred_element_type=jnp.float32))

    @pl.when(j >= nu_ref[0])
    def _():
        y_ref[...] = jnp.zeros_like(y_ref)


def _gmm(xs2d, n_rows, layer, w1, w3, w2, tile_expert, n_used):
    D, de = w1.shape[2], w1.shape[3]
    blk = pl.BlockSpec((TG * ROW_TILE, LANES), lambda j, te, nu: (j, 0))
    grid_spec = pltpu.PrefetchScalarGridSpec(
        num_scalar_prefetch=2,
        grid=(n_rows // TG,),
        in_specs=[
            blk,
            pl.BlockSpec((1, 1, D, de), lambda j, te, nu: (layer, te[j], 0, 0)),
            pl.BlockSpec((1, 1, D, de), lambda j, te, nu: (layer, te[j], 0, 0)),
            pl.BlockSpec((1, 1, de, D), lambda j, te, nu: (layer, te[j], 0, 0)),
        ],
        out_specs=blk,
        scratch_shapes=[pltpu.VMEM((D, de), jnp.bfloat16), pltpu.VMEM((D, de), jnp.bfloat16),
                        pltpu.VMEM((de, D), jnp.bfloat16)],
    )
    return pl.pallas_call(
        _gmm_kernel,
        grid_spec=grid_spec,
        out_shape=jax.ShapeDtypeStruct((n_rows * ROW_TILE, LANES), jnp.float32),
        compiler_params=_cparams(("arbitrary",)),
        name="expert_mlp",
    )(tile_expert, n_used, xs2d, w1, w3, w2)


def _combine_kernel(n_steps, idx_ref, idx_next_ref, h_ref, info_ref, g_ref, b_ref, ys_ref, o_ref, ybuf_ref, sem):
    step = pl.program_id(0)

    def gather(idx, slot):
        def row_copy(r):
            return pltpu.make_async_copy(ys_ref.at[idx[0, 0, r]],
                                         ybuf_ref.at[slot, pl.ds(r * ROW_TILE, ROW_TILE), :], sem.at[slot])
        return lambda i: (row_copy(2 * i), row_copy(2 * i + 1))

    def start(copy_pair):
        def body(i, c):
            first, second = copy_pair(i)
            first.start(priority=0)
            second.start(priority=1)
            return c
        lax.fori_loop(0, TM, body, 0, unroll=4)

    def wait(copy_pair):
        def body(i, c):
            first, second = copy_pair(i)
            first.wait()
            second.wait()
            return c
        lax.fori_loop(0, TM, body, 0, unroll=4)

    @pl.when(step == 0)
    def _():
        start(gather(idx_ref, 0))

    for slot in range(2):
        @pl.when((step & 1) == slot)
        def _():
            @pl.when(step + 1 < n_steps)
            def _():
                start(gather(idx_next_ref, 1 - slot))

            wait(gather(idx_ref, slot))
            rows = ybuf_ref.at[slot]
            y0 = jnp.concatenate([rows[pl.ds(j, TM, stride=ROW_TILE), :] for j in range(ROW_TILE)], axis=1)
            y1 = jnp.concatenate([rows[pl.ds(TM * ROW_TILE + j, TM, stride=ROW_TILE), :]
                                  for j in range(ROW_TILE)], axis=1)
            info = info_ref[...]
            f = info[:, 2:3] * y0 + info[:, 3:4] * y1
            o_ref[...] = _layernorm(ALPHA * h_ref[...] + f, g_ref[...], b_ref[...])


def _combine_ln(h2d, ys, pos, info, g, b, batch, drop_front):
    T, D = h2d.shape
    nt = T // TM
    ntb = nt // batch
    if drop_front:
        n_steps = batch * (ntb - 1)
        tile = lambda s: s + s // (ntb - 1) + 1
    else:
        n_steps = nt
        tile = lambda s: s
    idx = jnp.concatenate([pos[:, 0].reshape(nt, TM), pos[:, 1].reshape(nt, TM)], axis=1).reshape(nt, 1, 2 * TM)
    idx_spec = lambda f: pl.BlockSpec((1, 1, 2 * TM), lambda s: (f(s), 0, 0), memory_space=pltpu.SMEM)
    return pl.pallas_call(
        functools.partial(_combine_kernel, n_steps),
        grid=(n_steps,),
        in_specs=[
            idx_spec(tile),
            idx_spec(lambda s: tile(jnp.minimum(s + 1, n_steps - 1))),
            pl.BlockSpec((TM, D), lambda s: (tile(s), 0)),
            pl.BlockSpec((TM, LANES), lambda s: (tile(s), 0)),
            pl.BlockSpec((1, D), lambda s: (0, 0)),
            pl.BlockSpec((1, D), lambda s: (0, 0)),
            pl.BlockSpec(memory_space=pl.ANY),
        ],
        out_specs=pl.BlockSpec((TM, D), lambda s: (s, 0)),
        out_shape=jax.ShapeDtypeStruct((n_steps * TM, D), jnp.float32),
        scratch_shapes=[pltpu.VMEM((2, 2 * TM * ROW_TILE, LANES), jnp.float32), pltpu.SemaphoreType.DMA((2,))],
        compiler_params=_cparams(("arbitrary",)),
        name="moe_combine_ln",
    )(idx, idx, h2d, info, g.reshape(1, D), b.reshape(1, D), ys)


def _moe_layer(h2d, layer, wg, bg, we, be, w1, w3, w2, pg, pb, batch, drop_front):
    T, D = h2d.shape
    n_rows = 2 * T + N_EXPERTS * TG
    info, cnt = _route(h2d, wg, bg, we, be)
    pos, pad_rows, tile_expert, n_used = _dispatch_plan(info, cnt, n_rows)
    xs = _dispatch_rows(h2d, pos, pad_rows, n_used, n_rows)
    ys = _gmm(xs.reshape(-1, LANES), n_rows, layer, w1, w3, w2, tile_expert, n_used)
    return _combine_ln(h2d, ys.reshape(n_rows, ROW_TILE, LANES), pos, info, pg, pb, batch, drop_front)


def _qkv_kernel(h_ref, wq_ref, wk_ref, wv_ref, cosT_ref, sinT_ref, cn_ref, s1_ref, s2_ref,
                q_ref, k_ref, v_ref):
    hb = h_ref[0].astype(jnp.bfloat16)
    half = ROT_DIM // 2
    nt_dims = (((1,), (1,)), ((), ()))
    qT = lax.dot_general(wq_ref[...], hb, nt_dims, preferred_element_type=jnp.float32)
    cosT = cosT_ref[...]
    sinT = sinT_ref[...]
    scale = HEAD_DIM ** -0.5 * math.log2(math.e)
    for hh in range(N_HEADS):
        parts = []
        for c in range(2):
            r0 = hh * 2 * HEAD_DIM + c * HEAD_DIM
            t1 = qT[r0:r0 + half]
            t2 = qT[r0 + half:r0 + 2 * half]
            parts += [t1 * cosT - t2 * sinT, t1 * sinT + t2 * cosT, qT[r0 + 2 * half:r0 + HEAD_DIM]]
        q_ref[0, hh, 0] = (jnp.concatenate(parts, axis=0) * scale).astype(jnp.bfloat16)
    k = jnp.dot(hb, wk_ref[...], preferred_element_type=jnp.float32)
    cn = cn_ref[...]
    s1 = s1_ref[...]
    s2 = s2_ref[...]
    for hh in range(N_HEADS):
        kb = k[:, hh * LANES:(hh + 1) * LANES]
        kr = kb * cn + pltpu.roll(kb, LANES - half, 1) * s1 + pltpu.roll(kb, half, 1) * s2
        k_ref[0, hh, 0] = kr.astype(jnp.bfloat16)
    vT = lax.dot_general(wv_ref[...], hb, nt_dims, preferred_element_type=jnp.float32)
    extra = lax.broadcasted_iota(jnp.int32, (V_ROWS - V_DIM, TM), 0)
    ones_row = jnp.where(extra == 0, 1.0, 0.0).astype(jnp.bfloat16)
    for hh in range(N_HEADS):
        v_ref[0, hh, 0, 0:V_DIM, :] = vT[hh * V_DIM:(hh + 1) * V_DIM].astype(jnp.bfloat16)
        v_ref[0, hh, 0, V_DIM:V_ROWS, :] = ones_row


def _rope_tables(lp):
    half = ROT_DIM // 2
    pos = jnp.arange(lp, dtype=jnp.float32) - PAD0
    inv = ROPE_THETA ** (-jnp.arange(0, ROT_DIM, 2, dtype=jnp.float32) / ROT_DIM)
    ang = pos[:, None] * inv[None, :]
    cos, sin = jnp.cos(ang), jnp.sin(ang)
    lane = jnp.arange(LANES) % HEAD_DIM
    is1 = lane < half
    is2 = (lane >= half) & (lane < ROT_DIM)
    fi = jnp.where(is1, lane, jnp.where(is2, lane - half, 0))
    cn = jnp.where(is1 | is2, cos[:, fi], 1.0)
    s1 = jnp.where(is1, -sin[:, fi], 0.0)
    s2 = jnp.where(is2, sin[:, fi], 0.0)
    return cos.T, sin.T, cn, s1, s2


def _qkv(h, wq, wk, wv):
    B, lp, D = h.shape
    nt = lp // TM
    cosT, sinT, cn, s1, s2 = _rope_tables(lp)
    full2 = lambda b, i: (0, 0)
    hd2 = 2 * HEAD_DIM
    out5 = lambda r, c: jax.ShapeDtypeStruct((B, N_HEADS, nt, r, c), jnp.bfloat16)
    spec5 = lambda r, c: pl.BlockSpec((1, N_HEADS, 1, r, c), lambda b, i: (b, 0, i, 0, 0))
    return pl.pallas_call(
        _qkv_kernel,
        grid=(B, nt),
        in_specs=[
            pl.BlockSpec((1, TM, D), lambda b, i: (b, i, 0)),
            pl.BlockSpec((D, D), full2),
            pl.BlockSpec((D, D), full2),
            pl.BlockSpec((D, D), full2),
            pl.BlockSpec((ROT_DIM // 2, TM), lambda b, i: (0, i)),
            pl.BlockSpec((ROT_DIM // 2, TM), lambda b, i: (0, i)),
            pl.BlockSpec((TM, LANES), lambda b, i: (i, 0)),
            pl.BlockSpec((TM, LANES), lambda b, i: (i, 0)),
            pl.BlockSpec((TM, LANES), lambda b, i: (i, 0)),
        ],
        out_specs=[spec5(hd2, TM), spec5(TM, hd2), spec5(V_ROWS, TM)],
        out_shape=[out5(hd2, TM), out5(TM, hd2), out5(V_ROWS, TM)],
        compiler_params=_cparams(("parallel", "parallel")),
        name="qkv_proj",
    )(h, wq.T.astype(jnp.bfloat16), wk.astype(jnp.bfloat16), wv.T.astype(jnp.bfloat16), cosT, sinT, cn, s1, s2)


MASK_PAD, MASK_DIAG, MASK_ALL = 1, 2, 4
ATT_UNROLLS = (16, 4, 2)
SOFTMAX_ROWS = 32


MASK_VALUE = -1e30
V_ROWS = V_DIM + 16
N_CHUNK_T = TM // CHUNK


def _attn_mask_tables():
    krow = jnp.arange(TM)[:, None]
    col = jnp.arange(LANES)[None, :]
    diag = (col < N_CHUNK_T) & (krow // CHUNK == col)
    pad = (col == N_CHUNK_T) & (krow < PAD0)
    everything = jnp.broadcast_to(col == N_CHUNK_T + 1, (TM, LANES))
    none = jnp.zeros((TM, LANES), bool)
    sel = jnp.stack([none, pad, diag, pad | diag, everything]).astype(jnp.bfloat16)
    row = jnp.arange(LANES)[:, None]
    qchunk = (jnp.arange(2 * TM)[None, :] % TM) // CHUNK
    hidden = ((row < N_CHUNK_T) & (row > qchunk)) | (row == N_CHUNK_T) | (row == N_CHUNK_T + 1)
    val = jnp.where(hidden, MASK_VALUE, 0.0).astype(jnp.bfloat16)
    return sel, val


def _attn_kernel(lam_init, q_ref, k_ref, v_ref, sel_ref, val_ref, lam_ref, g_ref, o_ref,
                 qcat_ref, s_ref, mt_ref, p_ref, alpha_ref, m_ref, acc_ref):
    qi = pl.program_id(2)
    W = 2 * TM
    F = 2 * HEAD_DIM
    q = q_ref[0, 0, 0]
    zero = jnp.zeros((HEAD_DIM, TM), jnp.bfloat16)
    qcat_ref[0:F, 0:TM] = jnp.concatenate([q[:HEAD_DIM], zero], axis=0)
    qcat_ref[0:F, TM:W] = jnp.concatenate([zero, q[HEAD_DIM:]], axis=0)
    qcat_ref[F:2 * F, :] = val_ref[...]
    s_ref[1] = jnp.full((TM, W), -jnp.inf, jnp.float32)
    p_ref[0] = jnp.zeros((TM, W), jnp.bfloat16)
    alpha_ref[0] = jnp.ones((8, W), jnp.float32)
    mt_ref[1] = jnp.full((8, W), -jnp.inf, jnp.float32)
    m_ref[...] = jnp.full(m_ref.shape, jnp.finfo(jnp.float32).min, jnp.float32)
    acc_ref[...] = jnp.zeros_like(acc_ref)

    def stages(t, cur):
        prv = 1 - cur
        kind = jnp.where(t > qi, MASK_ALL, jnp.where(t == 0, MASK_PAD, 0) + jnp.where(t == qi, MASK_DIAG, 0))
        lhs = jnp.concatenate([k_ref[0, 0, jnp.minimum(t, qi)], sel_ref[kind]], axis=1)
        s_new = jnp.dot(lhs, qcat_ref[...], preferred_element_type=jnp.float32)
        s_ref[cur] = s_new
        mt_ref[cur, 0:1, :] = jnp.max(s_new, axis=0, keepdims=True)
        vt = v_ref[0, 0, jnp.clip(t - 2, 0, qi)]
        pv = jnp.dot(vt, p_ref[cur], preferred_element_type=jnp.float32)
        acc_ref[...] = alpha_ref[cur, 0:1, :] * acc_ref[...] + pv
        m_old = m_ref[0:1, :]
        m_new = jnp.maximum(m_old, mt_ref[prv, 0:1, :])
        for r in range(0, TM, SOFTMAX_ROWS):
            p = jnp.exp2(s_ref[prv, r:r + SOFTMAX_ROWS, :] - m_new)
            p_ref[prv, r:r + SOFTMAX_ROWS, :] = p.astype(jnp.bfloat16)
        m_ref[0:1, :] = m_new
        alpha_ref[prv, 0:1, :] = jnp.exp2(m_old - m_new)

    def unrolled(first, unroll):
        def body(i, carry):
            for u in range(unroll):
                stages(first + i * unroll + u, u & 1)
            return carry
        return body

    n_iter = qi + 3
    done = 0
    for unroll in ATT_UNROLLS[:-1]:
        n = (n_iter - done) // unroll
        lax.fori_loop(0, n, unrolled(done, unroll), 0)
        done = done + n * unroll
    last = ATT_UNROLLS[-1]
    lax.fori_loop(0, (n_iter - done + last - 1) // last, unrolled(done, last), 0)

    lp = lam_ref[...]
    lam = (jnp.exp(jnp.sum(lp[0:1] * lp[1:2], axis=1, keepdims=True))
           - jnp.exp(jnp.sum(lp[2:3] * lp[3:4], axis=1, keepdims=True)) + lam_init)
    l = acc_ref[V_DIM:V_DIM + 1, :]
    l = jnp.where(l == 0.0, 1.0, l)
    acc = acc_ref[0:V_DIM, :]
    o = acc[:, 0:TM] / l[:, 0:TM] - lam * (acc[:, TM:W] / l[:, TM:W])
    ms = jnp.mean(o * o, axis=0, keepdims=True)
    y = o * lax.rsqrt(ms + EPS) * g_ref[...] * (1.0 - lam_init)
    o_ref[0] = y.astype(o_ref.dtype)


def _attention(q5, k5, v5, lam_params, subln_g, lam_init):
    B, H, nt, _, _ = q5.shape
    lp = nt * TM
    kernel = functools.partial(_attn_kernel, lam_init)
    sel, val = _attn_mask_tables()
    return pl.pallas_call(
        kernel,
        grid=(B, H, nt),
        in_specs=[
            pl.BlockSpec((1, 1, 1, 2 * HEAD_DIM, TM), lambda b, h, i: (b, h, i, 0, 0)),
            pl.BlockSpec((1, 1, nt, TM, 2 * HEAD_DIM), lambda b, h, i: (b, h, 0, 0, 0)),
            pl.BlockSpec((1, 1, nt, V_ROWS, TM), lambda b, h, i: (b, h, 0, 0, 0)),
            pl.BlockSpec((5, TM, LANES), lambda b, h, i: (0, 0, 0)),
            pl.BlockSpec((LANES, 2 * TM), lambda b, h, i: (0, 0)),
            pl.BlockSpec((4, HEAD_DIM), lambda b, h, i: (0, 0)),
            pl.BlockSpec((V_DIM, 1), lambda b, h, i: (0, 0)),
        ],
        out_specs=pl.BlockSpec((1, V_DIM, TM), lambda b, h, i: (b, h, i)),
        out_shape=jax.ShapeDtypeStruct((B, H * V_DIM, lp), jnp.bfloat16),
        scratch_shapes=[
            pltpu.VMEM((4 * HEAD_DIM, 2 * TM), jnp.bfloat16),
            pltpu.VMEM((2, TM, 2 * TM), jnp.float32),
            pltpu.VMEM((2, 8, 2 * TM), jnp.float32),
            pltpu.VMEM((2, TM, 2 * TM), jnp.bfloat16),
            pltpu.VMEM((2, 8, 2 * TM), jnp.float32),
            pltpu.VMEM((8, 2 * TM), jnp.float32),
            pltpu.VMEM((V_ROWS, 2 * TM), jnp.float32),
        ],
        compiler_params=_cparams(("parallel", "parallel", "arbitrary")),
        name="diff_attention",
    )(q5, k5, v5, sel, val, lam_params, subln_g.reshape(V_DIM, 1))


def _attn_out_kernel(o_ref, w_ref, h_ref, g_ref, b_ref, out_ref):
    m = lax.dot_general(o_ref[0], w_ref[...], (((0,), (0,)), ((), ())), preferred_element_type=jnp.float32)
    out_ref[0] = _layernorm(ALPHA * h_ref[0] + m, g_ref[...], b_ref[...])


def _attn_out(o, w_o, h, g, b):
    B, lp, D = h.shape
    nt = lp // TM
    full2 = lambda bb, i: (0, 0)
    tile = pl.BlockSpec((1, TM, D), lambda bb, i: (bb, i, 0))
    return pl.pallas_call(
        _attn_out_kernel,
        grid=(B, nt),
        in_specs=[pl.BlockSpec((1, D, TM), lambda bb, i: (bb, 0, i)), pl.BlockSpec((D, D), full2), tile,
                  pl.BlockSpec((1, D), full2), pl.BlockSpec((1, D), full2)],
        out_specs=tile,
        out_shape=jax.ShapeDtypeStruct((B, lp, D), jnp.float32),
        compiler_params=_cparams(("parallel", "parallel")),
        name="attn_out_ln",
    )(o, w_o.astype(jnp.bfloat16), h, g.reshape(1, D), b.reshape(1, D))


def kernel(x, meta_tokens, conv_w_pw1, conv_b_pw1, conv_w_dw, conv_b_dw, conv_ln_g, conv_ln_b, conv_w_pw2, conv_b_pw2, kv_w_k, kv_w_v, attn_w_q, attn_lam_q1, attn_lam_k1, attn_lam_q2, attn_lam_k2, attn_subln_g, attn_w_o, post_ln_g, post_ln_b, moe_w_group, moe_b_group, moe_w_expert, moe_b_expert, moe_w1, moe_w3, moe_w2):
    B, S, D = x.shape
    assert D == D_MODEL and S % TM == 0 and S % CHUNK == 0
    assert conv_w_pw1.shape[0] == 1 and attn_w_q.shape[0] == 1 and post_ln_g.shape[0] == DEPTH
    lp = S + TM

    def moe(h, l, drop_front):
        out = _moe_layer(h.reshape(B * lp, D), l, moe_w_group[l], moe_b_group[l], moe_w_expert[l],
                         moe_b_expert[l], moe_w1, moe_w3, moe_w2,
                         post_ln_g[l, 1], post_ln_b[l, 1], B, drop_front)
        return out.reshape(B, -1, D)

    h = _conv_layer(x, meta_tokens, conv_w_pw1[0], conv_b_pw1[0], conv_w_dw[0], conv_b_dw[0],
                    conv_ln_g[0], conv_ln_b[0], conv_w_pw2[0], conv_b_pw2[0],
                    post_ln_g[0, 0], post_ln_b[0, 0])
    h = moe(h, 0, False)
    q5, k5, v5 = _qkv(h, attn_w_q[0], kv_w_k, kv_w_v)
    lam_init = 0.8 - 0.6 * math.exp(-0.3 * 1)
    lam_params = jnp.stack([attn_lam_q1[0], attn_lam_k1[0], attn_lam_q2[0], attn_lam_k2[0]])
    o = _attention(q5, k5, v5, lam_params, attn_subln_g[0], lam_init)
    h = _attn_out(o, attn_w_o[0], h, post_ln_g[1, 0], post_ln_b[1, 0])
    return moe(h, 1, True)
```

```python
import functools
import math

import jax
import jax.numpy as jnp
from jax import lax
from jax.experimental import pallas as pl
from jax.experimental.pallas import tpu as pltpu

D_MODEL = 1024
N_META = 16
CHUNK = 64
DEPTH = 2
ALPHA = (2.0 * DEPTH) ** 0.25
CONV_W = 31
N_HEADS = 8
HEAD_DIM = 64
V_DIM = 128
ROT_DIM = 16
ROPE_THETA = 500000.0
N_GROUPS = 4
EXPERTS_PER_GROUP = 4
N_EXPERTS = 16
D_EXPERT = 512
EPS = 1e-5

TM = 256
PAD0 = TM - N_META
HALO = 32
CONV_SLACK = 8
TG = 256
ROW_TILE = 8
MOVE_ROWS = 1024
LANES = 128
NEG = -1e30
VMEM_LIMIT = 48 * 1024 * 1024


def _cparams(sem):
    return pltpu.CompilerParams(dimension_semantics=sem, vmem_limit_bytes=VMEM_LIMIT)


def _layernorm(x, g, b):
    mu = jnp.mean(x, axis=-1, keepdims=True)
    xc = x - mu
    var = jnp.mean(xc * xc, axis=-1, keepdims=True)
    return xc * lax.rsqrt(var + EPS) * g + b


def _sigmoid(x):
    return 1.0 / (1.0 + jnp.exp(-x))


def _front_tile(meta_ref):
    return jnp.concatenate([jnp.zeros((PAD0, D_MODEL), jnp.float32), meta_ref[...]], axis=0)


def _stream_tile(i, x_ref, meta_ref):
    return jnp.where(i == 0, _front_tile(meta_ref), x_ref[0])


def _conv_in_kernel(x_ref, meta_ref, w_ref, b_ref, u_ref):
    i = pl.program_id(1)
    xt = _stream_tile(i, x_ref, meta_ref)
    h = jnp.dot(xt.astype(jnp.bfloat16), w_ref[...], preferred_element_type=jnp.float32) + b_ref[...]
    u = h[:, :D_MODEL] * _sigmoid(h[:, D_MODEL:])
    row = lax.broadcasted_iota(jnp.int32, (TM, 1), 0)
    u_ref[0] = jnp.where((i == 0) & (row < PAD0), 0.0, u)


def _conv_mid_kernel(ucur_ref, uprev_ref, x_ref, meta_ref, wdw_ref, bdw_ref, lng_ref, lnb_ref,
                     w2_ref, b2_ref, pg_ref, pb_ref, h_ref, win_ref, part_ref, conv_ref):
    i = pl.program_id(1)
    win_ref[0:HALO, :] = jnp.where(i == 0, 0.0, uprev_ref[0])
    win_ref[HALO:HALO + TM, :] = ucur_ref[0]
    win_ref[HALO + TM:, :] = jnp.zeros((CONV_SLACK, D_MODEL), jnp.float32)
    off = HALO - (CONV_W - 1)
    rows = TM + 16
    for c in range(D_MODEL // LANES):
        cs = slice(c * LANES, (c + 1) * LANES)
        acc = jnp.zeros((TM, LANES), jnp.float32)
        for r in range(8):
            part = None
            for j in range(r, CONV_W, 8):
                term = win_ref[j - r:j - r + rows, cs] * wdw_ref[j:j + 1, cs]
                part = term if part is None else part + term
            slot = (c * 8 + r) % 2
            part_ref[slot] = part
            acc = acc + part_ref[slot, off + r:off + r + TM, :]
        conv_ref[:, cs] = acc
    y = conv_ref[...] + bdw_ref[...]
    y = _layernorm(y, lng_ref[...], lnb_ref[...])
    y = y * _sigmoid(y)
    m = jnp.dot(y.astype(jnp.bfloat16), w2_ref[...], preferred_element_type=jnp.float32) + b2_ref[...]
    h0 = _stream_tile(i, x_ref, meta_ref)
    h_ref[0] = _layernorm(ALPHA * h0 + m, pg_ref[...], pb_ref[...])


def _conv_layer(x, meta, w1, b1, wdw, bdw, lng, lnb, w2, b2, pg, pb):
    B, S, D = x.shape
    nt = S // TM + 1
    lp = nt * TM
    xmap = lambda b, i: (b, jnp.maximum(i - 1, 0), 0)
    full2 = lambda b, i: (0, 0)
    u = pl.pallas_call(
        _conv_in_kernel,
        grid=(B, nt),
        in_specs=[
            pl.BlockSpec((1, TM, D), xmap),
            pl.BlockSpec((N_META, D), full2),
            pl.BlockSpec((D, 2 * D), full2),
            pl.BlockSpec((1, 2 * D), full2),
        ],
        out_specs=pl.BlockSpec((1, TM, D), lambda b, i: (b, i, 0)),
        out_shape=jax.ShapeDtypeStruct((B, lp, D), jnp.float32),
        compiler_params=_cparams(("parallel", "parallel")),
        name="conv_in",
    )(x, meta, w1.astype(jnp.bfloat16), b1.reshape(1, -1))
    wdw_p = jnp.concatenate([wdw, jnp.zeros((HALO - CONV_W, D), wdw.dtype)], axis=0)
    vec = lambda a: a.reshape(1, D)
    h1 = pl.pallas_call(
        _conv_mid_kernel,
        grid=(B, nt),
        in_specs=[
            pl.BlockSpec((1, TM, D), lambda b, i: (b, i, 0)),
            pl.BlockSpec((1, HALO, D), lambda b, i: (b, jnp.maximum(i * (TM // HALO) - 1, 0), 0)),
            pl.BlockSpec((1, TM, D), xmap),
            pl.BlockSpec((N_META, D), full2),
            pl.BlockSpec((HALO, D), full2),
            pl.BlockSpec((1, D), full2),
            pl.BlockSpec((1, D), full2),
            pl.BlockSpec((1, D), full2),
            pl.BlockSpec((D, D), full2),
            pl.BlockSpec((1, D), full2),
            pl.BlockSpec((1, D), full2),
            pl.BlockSpec((1, D), full2),
        ],
        out_specs=pl.BlockSpec((1, TM, D), lambda b, i: (b, i, 0)),
        out_shape=jax.ShapeDtypeStruct((B, lp, D), jnp.float32),
        scratch_shapes=[
            pltpu.VMEM((HALO + TM + CONV_SLACK, D), jnp.float32),
            pltpu.VMEM((2, TM + 16, LANES), jnp.float32),
            pltpu.VMEM((TM, D), jnp.float32),
        ],
        compiler_params=_cparams(("parallel", "parallel")),
        name="conv_mid",
    )(u, u, x, meta, wdw_p, vec(bdw), vec(lng), vec(lnb), w2.astype(jnp.bfloat16), vec(b2), vec(pg), vec(pb))
    return h1


def _route_kernel(h_ref, whi_ref, wlo_ref, b_ref, info_ref, cnt_ref, htile_ref, carry_ref):
    step = pl.program_id(0)

    @pl.when(step == 0)
    def _():
        carry_ref[...] = jnp.zeros_like(carry_ref)

    h = h_ref[...]
    _to_token_tiles(htile_ref, h)
    hhi = h.astype(jnp.bfloat16)
    hlo = (h - hhi.astype(jnp.float32)).astype(jnp.bfloat16)
    lg = (jnp.dot(hhi, whi_ref[...], preferred_element_type=jnp.float32)
          + jnp.dot(hlo, whi_ref[...], preferred_element_type=jnp.float32)
          + jnp.dot(hhi, wlo_ref[...], preferred_element_type=jnp.float32)) + b_ref[...]
    lane_i = lax.broadcasted_iota(jnp.int32, (TM, LANES), 1)
    lane = lane_i.astype(jnp.float32)

    def first_argmax(v, vmax):
        return jnp.min(jnp.where(v == vmax, lane, float(LANES)), axis=1, keepdims=True)

    gmask = lane_i < N_GROUPS
    glog = jnp.where(gmask, lg, NEG)
    gmax = jnp.max(glog, axis=1, keepdims=True)
    gidx = first_argmax(glog, gmax)
    gsum = jnp.sum(jnp.where(gmask, jnp.exp(glog - gmax), 0.0), axis=1, keepdims=True)
    g_w = 1.0 / gsum
    elane = lane_i - N_GROUPS
    egrp = (elane >> 2).astype(jnp.float32)
    emask = (elane >= 0) & (elane < N_EXPERTS) & (egrp == gidx)
    elog = jnp.where(emask, lg, NEG)
    e1 = jnp.max(elog, axis=1, keepdims=True)
    i1 = first_argmax(elog, e1)
    elog2 = jnp.where(lane == i1, NEG, elog)
    e2 = jnp.max(elog2, axis=1, keepdims=True)
    i2 = first_argmax(elog2, e2)
    d = jnp.exp(e2 - e1)
    w1 = g_w / (1.0 + d)
    w2 = g_w * d / (1.0 + d)
    x1 = i1 - float(N_GROUPS)
    x2 = i2 - float(N_GROUPS)
    oh1 = jnp.where(lane == x1, 1.0, 0.0)
    oh2 = jnp.where(lane == x2, 1.0, 0.0)
    r_i = lax.broadcasted_iota(jnp.int32, (TM, TM), 0)
    c_i = lax.broadcasted_iota(jnp.int32, (TM, TM), 1)
    ltri = jnp.where(c_i < r_i, 1.0, 0.0).astype(jnp.bfloat16)
    pre1 = jnp.dot(ltri, oh1.astype(jnp.bfloat16), preferred_element_type=jnp.float32)
    pre2 = jnp.dot(ltri, oh2.astype(jnp.bfloat16), preferred_element_type=jnp.float32)
    cnt1 = jnp.sum(oh1, axis=0, keepdims=True)
    cnt2 = jnp.sum(oh2, axis=0, keepdims=True)
    carry = carry_ref[0:1, :]
    rank1 = jnp.sum(oh1 * (carry + pre1), axis=1, keepdims=True)
    rank2 = jnp.sum(oh2 * (carry + cnt1 + pre2), axis=1, keepdims=True)
    new_carry = carry + cnt1 + cnt2
    carry_ref[...] = jnp.broadcast_to(new_carry, carry_ref.shape)
    cnt_ref[...] = jnp.broadcast_to(new_carry, cnt_ref.shape)
    info = jnp.where(lane_i == 0, x1, 0.0)
    info = jnp.where(lane_i == 1, x2, info)
    info = jnp.where(lane_i == 2, w1, info)
    info = jnp.where(lane_i == 3, w2, info)
    info = jnp.where(lane_i == 4, rank1, info)
    info = jnp.where(lane_i == 5, rank2, info)
    info_ref[...] = info


def _route(h2d, w_group, b_group, w_expert, b_expert):
    T, D = h2d.shape
    wr = jnp.zeros((D, LANES), jnp.float32)
    wr = wr.at[:, :N_GROUPS].set(w_group).at[:, N_GROUPS:N_GROUPS + N_EXPERTS].set(w_expert)
    br = jnp.zeros((1, LANES), jnp.float32)
    br = br.at[0, :N_GROUPS].set(b_group).at[0, N_GROUPS:N_GROUPS + N_EXPERTS].set(b_expert)
    whi = wr.astype(jnp.bfloat16)
    wlo = (wr - whi.astype(jnp.float32)).astype(jnp.bfloat16)
    return pl.pallas_call(
        _route_kernel,
        grid=(T // TM,),
        in_specs=[
            pl.BlockSpec((TM, D), lambda i: (i, 0)),
            pl.BlockSpec((D, LANES), lambda i: (0, 0)),
            pl.BlockSpec((D, LANES), lambda i: (0, 0)),
            pl.BlockSpec((1, LANES), lambda i: (0, 0)),
        ],
        out_specs=[
            pl.BlockSpec((TM, LANES), lambda i: (i, 0)),
            pl.BlockSpec((8, LANES), lambda i: (0, 0)),
            pl.BlockSpec((TM * ROW_TILE, LANES), lambda i: (i, 0)),
        ],
        out_shape=[
            jax.ShapeDtypeStruct((T, LANES), jnp.float32),
            jax.ShapeDtypeStruct((8, LANES), jnp.float32),
            jax.ShapeDtypeStruct((T * ROW_TILE, LANES), jnp.float32),
        ],
        scratch_shapes=[pltpu.VMEM((8, LANES), jnp.float32)],
        compiler_params=_cparams(("arbitrary",)),
        name="route",
    )(h2d, whi, wlo, br)


def _dispatch_plan(info, cnt, n_rows):
    ex = info[:, 0:2].astype(jnp.int32)
    rank = info[:, 4:6].astype(jnp.int32)
    counts = cnt[0, :N_EXPERTS].astype(jnp.int32)
    padded = ((counts + TG - 1) // TG) * TG
    ends = jnp.cumsum(padded)
    starts = ends - padded
    pos = starts[ex] + rank
    i = jnp.arange(TG, dtype=jnp.int32)[None, :]
    pad_rows = jnp.where(i < (padded - counts)[:, None], (starts + counts)[:, None] + i, -1)
    tile_start = jnp.arange(n_rows // TG, dtype=jnp.int32) * TG
    tile_expert = jnp.minimum(
        jnp.sum((tile_start[:, None] >= ends[None, :]).astype(jnp.int32), axis=1), N_EXPERTS - 1)
    n_used = (ends[-1] // TG).astype(jnp.int32).reshape(1)
    return pos, pad_rows, tile_expert.astype(jnp.int32), n_used


def _to_token_tiles(ref, x):
    rows = x.shape[0]
    for j in range(ROW_TILE):
        ref[pl.ds(j, rows, stride=ROW_TILE), :] = x[:, j * LANES:(j + 1) * LANES]


def _from_token_tiles(ref, rows):
    return jnp.concatenate([ref[pl.ds(j, rows, stride=ROW_TILE), :] for j in range(ROW_TILE)], axis=1)


def _issue_row_copies(copy_pair, n_pairs):
    def start(i, c):
        a, b = copy_pair(i)
        a.start(priority=0)
        b.start(priority=1)
        return c

    def wait(i, c):
        a, b = copy_pair(i)
        a.wait()
        b.wait()
        return c

    lax.fori_loop(0, n_pairs, start, 0, unroll=4)
    lax.fori_loop(0, n_pairs, wait, 0, unroll=4)


def _dispatch_kernel(n_tiles, pads_per_step, tails_per_step, dst_ref, pad_ref, nu_ref, table_ref,
                     out_ref, zrow_ref, ztile_ref, sem, zsem):
    step = pl.program_id(0)

    @pl.when(step == 0)
    def _():
        zrow_ref[...] = jnp.zeros_like(zrow_ref)
        ztile_ref[...] = jnp.zeros_like(ztile_ref)

    def row_copy(i, k):
        return pltpu.make_async_copy(table_ref.at[i], out_ref.at[dst_ref[0, 0, 2 * i + k]], sem)

    _issue_row_copies(lambda i: (row_copy(i, 0), row_copy(i, 1)), MOVE_ROWS // 2)

    def tail_copy(k):
        tail = nu_ref[0] + step * tails_per_step + k
        copy = pltpu.make_async_copy(ztile_ref, out_ref.at[pl.ds(jnp.minimum(tail, n_tiles - 1) * TG, TG)], zsem)
        return tail < n_tiles, copy

    def pad_copy(r):
        return pltpu.make_async_copy(zrow_ref, out_ref.at[jnp.maximum(pad_ref[0, 0, r], 0)], zsem)

    def start(r, c):
        @pl.when(pad_ref[0, 0, r] >= 0)
        def _():
            pad_copy(r).start()
        return c

    def wait(r, c):
        @pl.when(pad_ref[0, 0, r] >= 0)
        def _():
            pad_copy(r).wait()
        return c

    for k in range(tails_per_step):
        live, copy = tail_copy(k)
        pl.when(live)(copy.start)
    lax.fori_loop(0, pads_per_step, start, 0)
    lax.fori_loop(0, pads_per_step, wait, 0)
    for k in range(tails_per_step):
        live, copy = tail_copy(k)
        pl.when(live)(copy.wait)


def _dispatch_rows(table, dst, pad_rows, n_used, n_out):
    n = dst.size
    steps = n // MOVE_ROWS
    pads_per_step = pl.cdiv(pad_rows.size, steps)
    tails_per_step = pl.cdiv(n_out // TG - n // TG, steps)
    pads = jnp.concatenate([pad_rows.reshape(-1),
                            jnp.full((steps * pads_per_step - pad_rows.size,), -1, jnp.int32)])
    return pl.pallas_call(
        functools.partial(_dispatch_kernel, n_out // TG, pads_per_step, tails_per_step),
        grid=(steps,),
        in_specs=[
            pl.BlockSpec((1, 1, MOVE_ROWS), lambda i: (i, 0, 0), memory_space=pltpu.SMEM),
            pl.BlockSpec((1, 1, pads_per_step), lambda i: (i, 0, 0), memory_space=pltpu.SMEM),
            pl.BlockSpec(memory_space=pltpu.SMEM),
            pl.BlockSpec((MOVE_ROWS // 2,) + table.shape[1:], lambda i: (i, 0, 0)),
        ],
        out_specs=pl.BlockSpec(memory_space=pl.ANY),
        out_shape=jax.ShapeDtypeStruct((n_out,) + table.shape[1:], table.dtype),
        scratch_shapes=[
            pltpu.VMEM(table.shape[1:], table.dtype),
            pltpu.VMEM((TG,) + table.shape[1:], table.dtype),
            pltpu.SemaphoreType.DMA(()),
            pltpu.SemaphoreType.DMA(()),
        ],
        compiler_params=_cparams(("arbitrary",)),
        name="dispatch_rows",
    )(dst.reshape(steps, 1, MOVE_ROWS), pads.reshape(steps, 1, pads_per_step), n_used, table)


def _gmm_kernel(te_ref, nu_ref, x_ref, w1_ref, w3_ref, w2_ref, y_ref, w1b_ref, w3b_ref, w2b_ref):
    j = pl.program_id(0)

    @pl.when((j == 0) | (te_ref[j] != te_ref[jnp.maximum(j - 1, 0)]))
    def _():
        w1b_ref[...] = w1_ref[0, 0].astype(jnp.bfloat16)
        w3b_ref[...] = w3_ref[0, 0].astype(jnp.bfloat16)
        w2b_ref[...] = w2_ref[0, 0].astype(jnp.bfloat16)

    @pl.when(j < nu_ref[0])
    def _():
        x = _from_token_tiles(x_ref, TG).astype(jnp.bfloat16)
        a = jnp.dot(x, w1b_ref[...], preferred_element_type=jnp.float32)
        g = jnp.dot(x, w3b_ref[...], preferred_element_type=jnp.float32)
        hh = (a * _sigmoid(a) * g).astype(jnp.bfloat16)
        _to_token_tiles(y_ref, jnp.dot(hh, w2b_ref[...], preferred_element_type=jnp.float32))

    @pl.when(j >= nu_ref[0])
    def _():
        y_ref[...] = jnp.zeros_like(y_ref)


def _gmm(xs2d, n_rows, layer, w1, w3, w2, tile_expert, n_used):
    D, de = w1.shape[2], w1.shape[3]
    blk = pl.BlockSpec((TG * ROW_TILE, LANES), lambda j, te, nu: (j, 0))
    grid_spec = pltpu.PrefetchScalarGridSpec(
        num_scalar_prefetch=2,
        grid=(n_rows // TG,),
        in_specs=[
            blk,
            pl.BlockSpec((1, 1, D, de), lambda j, te, nu: (layer, te[j], 0, 0)),
            pl.BlockSpec((1, 1, D, de), lambda j, te, nu: (layer, te[j], 0, 0)),
            pl.BlockSpec((1, 1, de, D), lambda j, te, nu: (layer, te[j], 0, 0)),
        ],
        out_specs=blk,
        scratch_shapes=[pltpu.VMEM((D, de), jnp.bfloat16), pltpu.VMEM((D, de), jnp.bfloat16),
                        pltpu.VMEM((de, D), jnp.bfloat16)],
    )
    return pl.pallas_call(
        _gmm_kernel,
        grid_spec=grid_spec,
        out_shape=jax.ShapeDtypeStruct((n_rows * ROW_TILE, LANES), jnp.float32),
        compiler_params=_cparams(("arbitrary",)),
        name="expert_mlp",
    )(tile_expert, n_used, xs2d, w1, w3, w2)


def _combine_kernel(n_steps, idx_ref, idx_next_ref, h_ref, info_ref, g_ref, b_ref, ys_ref, o_ref, ybuf_ref, sem):
    step = pl.program_id(0)

    def gather(idx, slot):
        def row_copy(r):
            return pltpu.make_async_copy(ys_ref.at[idx[0, 0, r]],
                                         ybuf_ref.at[slot, pl.ds(r * ROW_TILE, ROW_TILE), :], sem.at[slot])
        return lambda i: (row_copy(2 * i), row_copy(2 * i + 1))

    def start(copy_pair):
        def body(i, c):
            first, second = copy_pair(i)
            first.start(priority=0)
            second.start(priority=1)
            return c
        lax.fori_loop(0, TM, body, 0, unroll=4)

    def wait(copy_pair):
        def body(i, c):
            first, second = copy_pair(i)
            first.wait()
            second.wait()
            return c
        lax.fori_loop(0, TM, body, 0, unroll=4)

    @pl.when(step == 0)
    def _():
        start(gather(idx_ref, 0))

    for slot in range(2):
        @pl.when((step & 1) == slot)
        def _():
            @pl.when(step + 1 < n_steps)
            def _():
                start(gather(idx_next_ref, 1 - slot))

            wait(gather(idx_ref, slot))
            rows = ybuf_ref.at[slot]
            y0 = jnp.concatenate([rows[pl.ds(j, TM, stride=ROW_TILE), :] for j in range(ROW_TILE)], axis=1)
            y1 = jnp.concatenate([rows[pl.ds(TM * ROW_TILE + j, TM, stride=ROW_TILE), :]
                                  for j in range(ROW_TILE)], axis=1)
            info = info_ref[...]
            f = info[:, 2:3] * y0 + info[:, 3:4] * y1
            o_ref[...] = _layernorm(ALPHA * h_ref[...] + f, g_ref[...], b_ref[...])


def _combine_ln(h2d, ys, pos, info, g, b, batch, drop_front):
    T, D = h2d.shape
    nt = T // TM
    ntb = nt // batch
    if drop_front:
        n_steps = batch * (ntb - 1)
        tile = lambda s: s + s // (ntb - 1) + 1
    else:
        n_steps = nt
        tile = lambda s: s
    idx = jnp.concatenate([pos[:, 0].reshape(nt, TM), pos[:, 1].reshape(nt, TM)], axis=1).reshape(nt, 1, 2 * TM)
    idx_spec = lambda f: pl.BlockSpec((1, 1, 2 * TM), lambda s: (f(s), 0, 0), memory_space=pltpu.SMEM)
    return pl.pallas_call(
        functools.partial(_combine_kernel, n_steps),
        grid=(n_steps,),
        in_specs=[
            idx_spec(tile),
            idx_spec(lambda s: tile(jnp.minimum(s + 1, n_steps - 1))),
            pl.BlockSpec((TM, D), lambda s: (tile(s), 0)),
            pl.BlockSpec((TM, LANES), lambda s: (tile(s), 0)),
            pl.BlockSpec((1, D), lambda s: (0, 0)),
            pl.BlockSpec((1, D), lambda s: (0, 0)),
            pl.BlockSpec(memory_space=pl.ANY),
        ],
        out_specs=pl.BlockSpec((TM, D), lambda s: (s, 0)),
        out_shape=jax.ShapeDtypeStruct((n_steps * TM, D), jnp.float32),
        scratch_shapes=[pltpu.VMEM((2, 2 * TM * ROW_TILE, LANES), jnp.float32), pltpu.SemaphoreType.DMA((2,))],
        compiler_params=_cparams(("arbitrary",)),
        name="moe_combine_ln",
    )(idx, idx, h2d, info, g.reshape(1, D), b.reshape(1, D), ys)


def _moe_layer(h2d, layer, wg, bg, we, be, w1, w3, w2, pg, pb, batch, drop_front):
    T, D = h2d.shape
    n_rows = 2 * T + N_EXPERTS * TG
    info, cnt, h_tiles = _route(h2d, wg, bg, we, be)
    pos, pad_rows, tile_expert, n_used = _dispatch_plan(info, cnt, n_rows)
    xs = _dispatch_rows(h_tiles.reshape(T, ROW_TILE, LANES), pos, pad_rows, n_used, n_rows)
    ys = _gmm(xs.reshape(-1, LANES), n_rows, layer, w1, w3, w2, tile_expert, n_used)
    return _combine_ln(h2d, ys.reshape(n_rows, ROW_TILE, LANES), pos, info, pg, pb, batch, drop_front)


def _qkv_kernel(h_ref, wq_ref, wk_ref, wv_ref, cosT_ref, sinT_ref, cn_ref, s1_ref, s2_ref,
                q_ref, k_ref, v_ref):
    hb = h_ref[0].astype(jnp.bfloat16)
    half = ROT_DIM // 2
    nt_dims = (((1,), (1,)), ((), ()))
    qT = lax.dot_general(wq_ref[...], hb, nt_dims, preferred_element_type=jnp.float32)
    cosT = cosT_ref[...]
    sinT = sinT_ref[...]
    scale = HEAD_DIM ** -0.5 * math.log2(math.e)
    for hh in range(N_HEADS):
        parts = []
        for c in range(2):
            r0 = hh * 2 * HEAD_DIM + c * HEAD_DIM
            t1 = qT[r0:r0 + half]
            t2 = qT[r0 + half:r0 + 2 * half]
            parts += [t1 * cosT - t2 * sinT, t1 * sinT + t2 * cosT, qT[r0 + 2 * half:r0 + HEAD_DIM]]
        q_ref[0, hh, 0] = (jnp.concatenate(parts, axis=0) * scale).astype(jnp.bfloat16)
    k = jnp.dot(hb, wk_ref[...], preferred_element_type=jnp.float32)
    cn = cn_ref[...]
    s1 = s1_ref[...]
    s2 = s2_ref[...]
    for hh in range(N_HEADS):
        kb = k[:, hh * LANES:(hh + 1) * LANES]
        kr = kb * cn + pltpu.roll(kb, LANES - half, 1) * s1 + pltpu.roll(kb, half, 1) * s2
        k_ref[0, hh, 0] = kr.astype(jnp.bfloat16)
    vT = lax.dot_general(wv_ref[...], hb, nt_dims, preferred_element_type=jnp.float32)
    extra = lax.broadcasted_iota(jnp.int32, (V_ROWS - V_DIM, TM), 0)
    ones_row = jnp.where(extra == 0, 1.0, 0.0).astype(jnp.bfloat16)
    for hh in range(N_HEADS):
        v_ref[0, hh, 0, 0:V_DIM, :] = vT[hh * V_DIM:(hh + 1) * V_DIM].astype(jnp.bfloat16)
        v_ref[0, hh, 0, V_DIM:V_ROWS, :] = ones_row


def _rope_tables(lp):
    half = ROT_DIM // 2
    pos = jnp.arange(lp, dtype=jnp.float32) - PAD0
    inv = ROPE_THETA ** (-jnp.arange(0, ROT_DIM, 2, dtype=jnp.float32) / ROT_DIM)
    ang = pos[:, None] * inv[None, :]
    cos, sin = jnp.cos(ang), jnp.sin(ang)
    lane = jnp.arange(LANES) % HEAD_DIM
    is1 = lane < half
    is2 = (lane >= half) & (lane < ROT_DIM)
    fi = jnp.where(is1, lane, jnp.where(is2, lane - half, 0))
    cn = jnp.where(is1 | is2, cos[:, fi], 1.0)
    s1 = jnp.where(is1, -sin[:, fi], 0.0)
    s2 = jnp.where(is2, sin[:, fi], 0.0)
    return cos.T, sin.T, cn, s1, s2


def _qkv(h, wq, wk, wv):
    B, lp, D = h.shape
    nt = lp // TM
    cosT, sinT, cn, s1, s2 = _rope_tables(lp)
    full2 = lambda b, i: (0, 0)
    hd2 = 2 * HEAD_DIM
    out5 = lambda r, c: jax.ShapeDtypeStruct((B, N_HEADS, nt, r, c), jnp.bfloat16)
    spec5 = lambda r, c: pl.BlockSpec((1, N_HEADS, 1, r, c), lambda b, i: (b, 0, i, 0, 0))
    return pl.pallas_call(
        _qkv_kernel,
        grid=(B, nt),
        in_specs=[
            pl.BlockSpec((1, TM, D), lambda b, i: (b, i, 0)),
            pl.BlockSpec((D, D), full2),
            pl.BlockSpec((D, D), full2),
            pl.BlockSpec((D, D), full2),
            pl.BlockSpec((ROT_DIM // 2, TM), lambda b, i: (0, i)),
            pl.BlockSpec((ROT_DIM // 2, TM), lambda b, i: (0, i)),
            pl.BlockSpec((TM, LANES), lambda b, i: (i, 0)),
            pl.BlockSpec((TM, LANES), lambda b, i: (i, 0)),
            pl.BlockSpec((TM, LANES), lambda b, i: (i, 0)),
        ],
        out_specs=[spec5(hd2, TM), spec5(TM, hd2), spec5(V_ROWS, TM)],
        out_shape=[out5(hd2, TM), out5(TM, hd2), out5(V_ROWS, TM)],
        compiler_params=_cparams(("parallel", "parallel")),
        name="qkv_proj",
    )(h, wq.T.astype(jnp.bfloat16), wk.astype(jnp.bfloat16), wv.T.astype(jnp.bfloat16), cosT, sinT, cn, s1, s2)


MASK_PAD, MASK_DIAG, MASK_ALL = 1, 2, 4
ATT_UNROLLS = (16, 8, 4, 2)
SOFTMAX_ROWS = 32


MASK_VALUE = -1e30
V_ROWS = V_DIM + 16
N_CHUNK_T = TM // CHUNK


def _attn_mask_tables():
    krow = jnp.arange(TM)[:, None]
    col = jnp.arange(LANES)[None, :]
    diag = (col < N_CHUNK_T) & (krow // CHUNK == col)
    pad = (col == N_CHUNK_T) & (krow < PAD0)
    everything = jnp.broadcast_to(col == N_CHUNK_T + 1, (TM, LANES))
    none = jnp.zeros((TM, LANES), bool)
    sel = jnp.stack([none, pad, diag, pad | diag, everything]).astype(jnp.bfloat16)
    row = jnp.arange(LANES)[:, None]
    qchunk = (jnp.arange(2 * TM)[None, :] % TM) // CHUNK
    hidden = ((row < N_CHUNK_T) & (row > qchunk)) | (row == N_CHUNK_T) | (row == N_CHUNK_T + 1)
    val = jnp.where(hidden, MASK_VALUE, 0.0).astype(jnp.bfloat16)
    return sel, val


def _attn_kernel(lam_init, q_ref, k_ref, v_ref, sel_ref, val_ref, lam_ref, g_ref, o_ref,
                 qcat_ref, s_ref, mt_ref, p_ref, alpha_ref, m_ref, acc_ref):
    qi = pl.program_id(2)
    W = 2 * TM
    F = 2 * HEAD_DIM
    q = q_ref[0, 0, 0]
    zero = jnp.zeros((HEAD_DIM, TM), jnp.bfloat16)
    qcat_ref[0:F, 0:TM] = jnp.concatenate([q[:HEAD_DIM], zero], axis=0)
    qcat_ref[0:F, TM:W] = jnp.concatenate([zero, q[HEAD_DIM:]], axis=0)
    qcat_ref[F:2 * F, :] = val_ref[...]
    s_ref[1] = jnp.full((TM, W), -jnp.inf, jnp.float32)
    p_ref[0] = jnp.zeros((TM, W), jnp.bfloat16)
    alpha_ref[0] = jnp.ones((8, W), jnp.float32)
    mt_ref[1] = jnp.full((8, W), -jnp.inf, jnp.float32)
    m_ref[...] = jnp.full(m_ref.shape, jnp.finfo(jnp.float32).min, jnp.float32)
    acc_ref[...] = jnp.zeros_like(acc_ref)

    def stages(t, cur):
        prv = 1 - cur
        kind = jnp.where(t > qi, MASK_ALL, jnp.where(t == 0, MASK_PAD, 0) + jnp.where(t == qi, MASK_DIAG, 0))
        lhs = jnp.concatenate([k_ref[0, 0, jnp.minimum(t, qi)], sel_ref[kind]], axis=1)
        s_new = jnp.dot(lhs, qcat_ref[...], preferred_element_type=jnp.float32)
        s_ref[cur] = s_new
        mt_ref[cur, 0:1, :] = jnp.max(s_new, axis=0, keepdims=True)
        vt = v_ref[0, 0, jnp.clip(t - 2, 0, qi)]
        pv = jnp.dot(vt, p_ref[cur], preferred_element_type=jnp.float32)
        acc_ref[...] = alpha_ref[cur, 0:1, :] * acc_ref[...] + pv
        m_old = m_ref[0:1, :]
        m_new = jnp.maximum(m_old, mt_ref[prv, 0:1, :])
        for r in range(0, TM, SOFTMAX_ROWS):
            p = jnp.exp2(s_ref[prv, r:r + SOFTMAX_ROWS, :] - m_new)
            p_ref[prv, r:r + SOFTMAX_ROWS, :] = p.astype(jnp.bfloat16)
        m_ref[0:1, :] = m_new
        alpha_ref[prv, 0:1, :] = jnp.exp2(m_old - m_new)

    def unrolled(first, unroll):
        def body(i, carry):
            for u in range(unroll):
                stages(first + i * unroll + u, u & 1)
            return carry
        return body

    n_iter = qi + 3
    done = 0
    for unroll in ATT_UNROLLS[:-1]:
        n = (n_iter - done) // unroll
        lax.fori_loop(0, n, unrolled(done, unroll), 0)
        done = done + n * unroll
    last = ATT_UNROLLS[-1]
    lax.fori_loop(0, (n_iter - done + last - 1) // last, unrolled(done, last), 0)

    lp = lam_ref[...]
    lam = (jnp.exp(jnp.sum(lp[0:1] * lp[1:2], axis=1, keepdims=True))
           - jnp.exp(jnp.sum(lp[2:3] * lp[3:4], axis=1, keepdims=True)) + lam_init)
    l = acc_ref[V_DIM:V_DIM + 1, :]
    l = jnp.where(l == 0.0, 1.0, l)
    acc = acc_ref[0:V_DIM, :]
    o = acc[:, 0:TM] / l[:, 0:TM] - lam * (acc[:, TM:W] / l[:, TM:W])
    ms = jnp.mean(o * o, axis=0, keepdims=True)
    y = o * lax.rsqrt(ms + EPS) * g_ref[...] * (1.0 - lam_init)
    o_ref[0] = y.astype(o_ref.dtype)


def _attention(q5, k5, v5, lam_params, subln_g, lam_init):
    B, H, nt, _, _ = q5.shape
    lp = nt * TM
    kernel = functools.partial(_attn_kernel, lam_init)
    sel, val = _attn_mask_tables()
    return pl.pallas_call(
        kernel,
        grid=(B, H, nt),
        in_specs=[
            pl.BlockSpec((1, 1, 1, 2 * HEAD_DIM, TM), lambda b, h, i: (b, h, i, 0, 0)),
            pl.BlockSpec((1, 1, nt, TM, 2 * HEAD_DIM), lambda b, h, i: (b, h, 0, 0, 0)),
            pl.BlockSpec((1, 1, nt, V_ROWS, TM), lambda b, h, i: (b, h, 0, 0, 0)),
            pl.BlockSpec((5, TM, LANES), lambda b, h, i: (0, 0, 0)),
            pl.BlockSpec((LANES, 2 * TM), lambda b, h, i: (0, 0)),
            pl.BlockSpec((4, HEAD_DIM), lambda b, h, i: (0, 0)),
            pl.BlockSpec((V_DIM, 1), lambda b, h, i: (0, 0)),
        ],
        out_specs=pl.BlockSpec((1, V_DIM, TM), lambda b, h, i: (b, h, i)),
        out_shape=jax.ShapeDtypeStruct((B, H * V_DIM, lp), jnp.bfloat16),
        scratch_shapes=[
            pltpu.VMEM((4 * HEAD_DIM, 2 * TM), jnp.bfloat16),
            pltpu.VMEM((2, TM, 2 * TM), jnp.float32),
            pltpu.VMEM((2, 8, 2 * TM), jnp.float32),
            pltpu.VMEM((2, TM, 2 * TM), jnp.bfloat16),
            pltpu.VMEM((2, 8, 2 * TM), jnp.float32),
            pltpu.VMEM((8, 2 * TM), jnp.float32),
            pltpu.VMEM((V_ROWS, 2 * TM), jnp.float32),
        ],
        compiler_params=_cparams(("parallel", "parallel", "arbitrary")),
        name="diff_attention",
    )(q5, k5, v5, sel, val, lam_params, subln_g.reshape(V_DIM, 1))


def _attn_out_kernel(o_ref, w_ref, h_ref, g_ref, b_ref, out_ref):
    m = lax.dot_general(o_ref[0], w_ref[...], (((0,), (0,)), ((), ())), preferred_element_type=jnp.float32)
    out_ref[0] = _layernorm(ALPHA * h_ref[0] + m, g_ref[...], b_ref[...])


def _attn_out(o, w_o, h, g, b):
    B, lp, D = h.shape
    nt = lp // TM
    full2 = lambda bb, i: (0, 0)
    tile = pl.BlockSpec((1, TM, D), lambda bb, i: (bb, i, 0))
    return pl.pallas_call(
        _attn_out_kernel,
        grid=(B, nt),
        in_specs=[pl.BlockSpec((1, D, TM), lambda bb, i: (bb, 0, i)), pl.BlockSpec((D, D), full2), tile,
                  pl.BlockSpec((1, D), full2), pl.BlockSpec((1, D), full2)],
        out_specs=tile,
        out_shape=jax.ShapeDtypeStruct((B, lp, D), jnp.float32),
        compiler_params=_cparams(("parallel", "parallel")),
        name="attn_out_ln",
    )(o, w_o.astype(jnp.bfloat16), h, g.reshape(1, D), b.reshape(1, D))


def kernel(x, meta_tokens, conv_w_pw1, conv_b_pw1, conv_w_dw, conv_b_dw, conv_ln_g, conv_ln_b, conv_w_pw2, conv_b_pw2, kv_w_k, kv_w_v, attn_w_q, attn_lam_q1, attn_lam_k1, attn_lam_q2, attn_lam_k2, attn_subln_g, attn_w_o, post_ln_g, post_ln_b, moe_w_group, moe_b_group, moe_w_expert, moe_b_expert, moe_w1, moe_w3, moe_w2):
    B, S, D = x.shape
    assert D == D_MODEL and S % TM == 0 and S % CHUNK == 0
    assert conv_w_pw1.shape[0] == 1 and attn_w_q.shape[0] == 1 and post_ln_g.shape[0] == DEPTH
    lp = S + TM

    def moe(h, l, drop_front):
        out = _moe_layer(h.reshape(B * lp, D), l, moe_w_group[l], moe_b_group[l], moe_w_expert[l],
                         moe_b_expert[l], moe_w1, moe_w3, moe_w2,
                         post_ln_g[l, 1], post_ln_b[l, 1], B, drop_front)
        return out.reshape(B, -1, D)

    h = _conv_layer(x, meta_tokens, conv_w_pw1[0], conv_b_pw1[0], conv_w_dw[0], conv_b_dw[0],
                    conv_ln_g[0], conv_ln_b[0], conv_w_pw2[0], conv_b_pw2[0],
                    post_ln_g[0, 0], post_ln_b[0, 0])
    h = moe(h, 0, False)
    q5, k5, v5 = _qkv(h, attn_w_q[0], kv_w_k, kv_w_v)
    lam_init = 0.8 - 0.6 * math.exp(-0.3 * 1)
    lam_params = jnp.stack([attn_lam_q1[0], attn_lam_k1[0], attn_lam_q2[0], attn_lam_k2[0]])
    o = _attention(q5, k5, v5, lam_params, attn_subln_g[0], lam_init)
    h = _attn_out(o, attn_w_o[0], h, post_ln_g[1, 0], post_ln_b[1, 0])
    return moe(h, 1, True)
```
